```python
import math
import jax, jax.numpy as jnp
from jax import lax
import numpy as np

D_MODEL = 1024
BATCH = 32
SEQ = 2048
DEPTH = 2

HEAD_DIM = 64
N_MIX_HEADS = D_MODEL // HEAD_DIM
CONV_WIDTH = D_MODEL // 4
CONV_GROUPS = CONV_WIDTH // HEAD_DIM
RET_HEADS = (N_MIX_HEADS - CONV_GROUPS) // 2
MOBA_HEADS = N_MIX_HEADS - CONV_GROUPS - RET_HEADS
RET_WIDTH = RET_HEADS * HEAD_DIM
MOBA_WIDTH = MOBA_HEADS * HEAD_DIM
IN_COLS = 4 * RET_WIDTH + 3 * MOBA_WIDTH + 3 * CONV_WIDTH
CONV_K = 3
RET_CHUNK = 128
MOBA_BLOCK = 256
MOBA_TOPK = 3
MOBA_Q_CHUNK = 16
REL_BUCKETS = 32
REL_MAX_DIST = 128
ROPE_BASE = 10000.0
D_FF = 256 * ((8 * D_MODEL // 3 + 255) // 256)
DN_ALPHA = (2.0 * DEPTH) ** 0.25
DN_BETA = (8.0 * DEPTH) ** -0.25
LN_EPS = 1e-5

kernel_name = "hybrid_retention_moba_shortconv_deepnorm"

F32 = jnp.float32


def layer_norm(x, g, b):
    xf = x.astype(F32)
    mu = jnp.mean(xf, axis=-1, keepdims=True)
    var = jnp.mean(jnp.square(xf - mu), axis=-1, keepdims=True)
    y = (xf - mu) * lax.rsqrt(var + LN_EPS)
    return (y * g.astype(F32) + b.astype(F32)).astype(x.dtype)


def causal_dwconv(x, w):
    K = w.shape[0]
    S = x.shape[1]
    xp = jnp.pad(x, ((0, 0), (K - 1, 0), (0, 0)))
    y = xp[:, 0:S] * w[0]
    for i in range(1, K):
        y = y + xp[:, i:i + S] * w[i]
    return y


def split_heads(t, n_heads):
    B, S, _ = t.shape
    return t.reshape(B, S, n_heads, HEAD_DIM).transpose(0, 2, 1, 3)


def merge_heads(t):
    B, H, S, dh = t.shape
    return t.transpose(0, 2, 1, 3).reshape(B, S, H * dh)


def rotary(t):
    S, dh = t.shape[2], t.shape[3]
    inv = ROPE_BASE ** (-jnp.arange(0, dh, 2, dtype=F32) / dh)
    ang = jnp.arange(S, dtype=F32)[:, None] * inv[None, :]
    cos, sin = jnp.cos(ang), jnp.sin(ang)
    t1, t2 = t[..., : dh // 2], t[..., dh // 2:]
    return jnp.concatenate([t1 * cos - t2 * sin, t1 * sin + t2 * cos], axis=-1)


def retention(q, k, v):
    B, H, S, dh = q.shape
    C = RET_CHUNK
    n = S // C
    log_g = jnp.log(1.0 - 2.0 ** (-5.0 - jnp.arange(H, dtype=F32)))
    idx = jnp.arange(C, dtype=F32)
    diff = idx[:, None] - idx[None, :]
    decay = jnp.where(diff >= 0, jnp.exp(jnp.maximum(diff, 0.0)[None] * log_g[:, None, None]), 0.0)
    xi = jnp.exp((idx + 1.0)[None, :] * log_g[:, None])
    zeta = jnp.exp((C - 1.0 - idx)[None, :] * log_g[:, None])
    g_chunk = jnp.exp(C * log_g)

    def to_chunks(t):
        return t.reshape(B, H, n, C, dh).transpose(2, 0, 1, 3, 4)

    def step(state, qkv):
        qc, kc, vc = qkv
        scores = jnp.einsum('bhcd,bhed->bhce', qc, kc) * decay[None]
        inner = jnp.einsum('bhce,bhev->bhcv', scores, vc)
        cross = jnp.einsum('bhcd,bhdv->bhcv', qc, state) * xi[None, :, :, None]
        state = state * g_chunk[None, :, None, None] + jnp.einsum(
            'bhcd,bhcv->bhdv', kc * zeta[None, :, :, None], vc)
        return state, inner + cross

    state0 = jnp.zeros((B, H, dh, dh), F32)
    _, out = lax.scan(step, state0, (to_chunks(q), to_chunks(k), to_chunks(v)))
    return out.transpose(1, 2, 0, 3, 4).reshape(B, H, S, dh)


def t5_bucket(dist):
    n = jnp.maximum(dist, 0)
    max_exact = REL_BUCKETS // 2
    nf = jnp.maximum(n, 1).astype(F32)
    large = max_exact + (jnp.log(nf / max_exact) / math.log(REL_MAX_DIST / max_exact)
                         * (REL_BUCKETS - max_exact)).astype(jnp.int32)
    large = jnp.minimum(large, REL_BUCKETS - 1)
    return jnp.where(n < max_exact, n, large)


def moba_attention(q, k, v, rel_bias):
    B, H, S, dh = q.shape
    L = MOBA_BLOCK
    nb = -(-S // L)
    pad = nb * L - S
    kp = jnp.pad(k, ((0, 0), (0, 0), (0, pad), (0, 0))).reshape(B, H, nb, L, dh)
    vp = jnp.pad(v, ((0, 0), (0, 0), (0, pad), (0, 0))).reshape(B, H, nb, L, dh)
    k_mean = jnp.mean(kp, axis=3)
    topk = min(MOBA_TOPK, nb)
    scale = dh ** -0.5
    rel_t = rel_bias.T.astype(F32)
    bi = jnp.arange(B)[:, None, None, None]
    hi = jnp.arange(H)[None, :, None, None]
    Qc = MOBA_Q_CHUNK
    nc = S // Qc
    q_chunks = q.reshape(B, H, nc, Qc, dh).transpose(2, 0, 1, 3, 4)

    def attend(args):
        qi, ci = args
        q_pos = ci * Qc + jnp.arange(Qc, dtype=jnp.int32)
        own = (ci * Qc) // L
        gate = jnp.einsum('bhqd,bhnd->bhqn', qi, k_mean)
        gate = jnp.where(jnp.arange(nb) < own, gate, -jnp.inf)
        _, sel = lax.top_k(gate, topk)
        sel_valid = jnp.arange(topk) < own
        kg = kp[bi, hi, sel]
        vg = vp[bi, hi, sel]
        k_pos_sel = sel[..., None] * L + jnp.arange(L, dtype=jnp.int32)
        b_sel = rel_t[hi[..., None], t5_bucket(q_pos[:, None, None] - k_pos_sel)]
        s_sel = jnp.einsum('bhqd,bhqjld->bhqjl', qi, kg) * scale + b_sel
        s_sel = jnp.where(sel_valid[:, None], s_sel, -jnp.inf)
        k_own = lax.dynamic_index_in_dim(kp, own, axis=2, keepdims=False)
        v_own = lax.dynamic_index_in_dim(vp, own, axis=2, keepdims=False)
        k_pos_own = own * L + jnp.arange(L, dtype=jnp.int32)
        dist_own = q_pos[:, None] - k_pos_own[None, :]
        b_own = rel_t[:, t5_bucket(dist_own)]
        s_own = jnp.einsum('bhqd,bhld->bhql', qi, k_own) * scale + b_own[None]
        s_own = jnp.where(dist_own >= 0, s_own, -jnp.inf)
        logits = jnp.concatenate([s_sel.reshape(B, H, Qc, topk * L), s_own], axis=-1)
        p = jax.nn.softmax(logits, axis=-1)
        p_sel = p[..., : topk * L].reshape(B, H, Qc, topk, L)
        p_own = p[..., topk * L:]
        return (jnp.einsum('bhqjl,bhqjld->bhqd', p_sel, vg)
                + jnp.einsum('bhql,bhld->bhqd', p_own, v_own))

    out = lax.map(attend, (q_chunks, jnp.arange(nc, dtype=jnp.int32)))
    return out.transpose(1, 2, 0, 3, 4).reshape(B, H, S, dh)


def head_group_norm(t):
    mu = jnp.mean(t, axis=-1, keepdims=True)
    var = jnp.mean(jnp.square(t - mu), axis=-1, keepdims=True)
    return (t - mu) * lax.rsqrt(var + LN_EPS)


def hybrid_mixer(h, w_in, conv_w, w_out, rel_bias):
    proj = h @ w_in
    o = np.cumsum([0, RET_WIDTH, RET_WIDTH, RET_WIDTH, RET_WIDTH,
                   MOBA_WIDTH, MOBA_WIDTH, MOBA_WIDTH, CONV_WIDTH, CONV_WIDTH, CONV_WIDTH])
    parts = [proj[..., int(o[i]):int(o[i + 1])] for i in range(10)]
    rq, rk, rv, rg, mq, mk, mv, cb, cc, ch = parts
    scale = HEAD_DIM ** -0.5
    q_r = rotary(split_heads(rq, RET_HEADS).astype(F32))
    k_r = rotary(split_heads(rk, RET_HEADS).astype(F32)) * scale
    v_r = split_heads(rv, RET_HEADS).astype(F32)
    y_ret = merge_heads(head_group_norm(retention(q_r, k_r, v_r))) * jax.nn.silu(rg.astype(F32))
    y_moba = merge_heads(moba_attention(split_heads(mq, MOBA_HEADS).astype(F32),
                                        split_heads(mk, MOBA_HEADS).astype(F32),
                                        split_heads(mv, MOBA_HEADS).astype(F32), rel_bias))
    y_conv = cb.astype(F32) * causal_dwconv(cc.astype(F32) * ch.astype(F32), conv_w.astype(F32))
    cat = jnp.concatenate([y_ret, y_moba, y_conv], axis=-1).astype(h.dtype)
    return cat @ w_out


def conv_ffn(h, w_up, ffn_conv_w, w_down):
    u = h @ w_up
    a, b = u[..., :D_FF], u[..., D_FF:]
    a = causal_dwconv(a, ffn_conv_w)
    return (jax.nn.gelu(a.astype(F32), approximate=False) * b.astype(F32)).astype(h.dtype) @ w_down


def setup_inputs(seed: int = 0) -> dict:
    key = jax.random.key(seed)
    ks = jax.random.split(key, 12)
    nrm = jax.random.normal
    x = nrm(ks[0], (BATCH, SEQ, D_MODEL), F32)
    w_in = nrm(ks[1], (DEPTH, D_MODEL, IN_COLS), F32) * D_MODEL ** -0.5
    conv_w = nrm(ks[2], (DEPTH, CONV_K, CONV_WIDTH), F32) * CONV_K ** -0.5
    w_out = nrm(ks[3], (DEPTH, D_MODEL, D_MODEL), F32) * (D_MODEL ** -0.5 * DN_BETA)
    ln1_g = 1.0 + 0.02 * nrm(ks[4], (DEPTH, D_MODEL), F32)
    ln1_b = 0.02 * nrm(ks[5], (DEPTH, D_MODEL), F32)
    w_up = nrm(ks[6], (DEPTH, D_MODEL, 2 * D_FF), F32) * D_MODEL ** -0.5
    ffn_conv_w = nrm(ks[7], (DEPTH, CONV_K, D_FF), F32) * CONV_K ** -0.5
    w_down = nrm(ks[8], (DEPTH, D_FF, D_MODEL), F32) * (D_FF ** -0.5 * DN_BETA)
    ln2_g = 1.0 + 0.02 * nrm(ks[9], (DEPTH, D_MODEL), F32)
    ln2_b = 0.02 * nrm(ks[10], (DEPTH, D_MODEL), F32)
    rel_bias = 0.2 * nrm(ks[11], (REL_BUCKETS, MOBA_HEADS), F32)
    return {"x": x, "w_in": w_in, "conv_w": conv_w, "w_out": w_out,
            "ln1_g": ln1_g, "ln1_b": ln1_b, "w_up": w_up, "ffn_conv_w": ffn_conv_w,
            "w_down": w_down, "ln2_g": ln2_g, "ln2_b": ln2_b, "rel_bias": rel_bias}


def reference(x, w_in, conv_w, w_out, ln1_g, ln1_b, w_up, ffn_conv_w, w_down, ln2_g, ln2_b, rel_bias):
    for l in range(DEPTH):
        mix = hybrid_mixer(x, w_in[l], conv_w[l], w_out[l], rel_bias)
        x = layer_norm(DN_ALPHA * x + mix, ln1_g[l], ln1_b[l])
        f = conv_ffn(x, w_up[l], ffn_conv_w[l], w_down[l])
        x = layer_norm(DN_ALPHA * x + f, ln2_g[l], ln2_b[l])
    return x
```

```python
import math

import jax
import jax.numpy as jnp
from jax import lax
from jax.experimental import pallas as pl
from jax.experimental.pallas import tpu as pltpu

F32 = jnp.float32
BF16 = jnp.bfloat16

D_MODEL = 1024
HEAD_DIM = 64
RET_HEADS = 6
MOBA_HEADS = 6
RET_WIDTH = RET_HEADS * HEAD_DIM
MOBA_WIDTH = MOBA_HEADS * HEAD_DIM
CONV_WIDTH = 256
IN_COLS = 4 * RET_WIDTH + 3 * MOBA_WIDTH + 3 * CONV_WIDTH
CONV_K = 3
MOBA_BLOCK = 256
MOBA_TOPK = 3
REL_BUCKETS = 32
REL_MAX_DIST = 128
ROPE_BASE = 10000.0
D_FF = 2816
DEPTH = 2
DN_ALPHA = (2.0 * DEPTH) ** 0.25
LN_EPS = 1e-5
QK_SCALE = HEAD_DIM ** -0.5

LANES = 128
SUBLANES = 8
HEADS_PER_SLAB = LANES // HEAD_DIM
N_SLABS = RET_WIDTH // LANES
ROW_TILE = 512
RET_CHUNK = 256
FF_CHUNK = 256
VMEM_LIMIT = 56 * 1024 * 1024

_O = [0]
for _w in (RET_WIDTH,) * 4 + (MOBA_WIDTH,) * 3 + (CONV_WIDTH,) * 3:
    _O.append(_O[-1] + _w)
O_RQ, O_RK, O_RV, O_RG, O_MQ, O_MK, O_MV, O_CB, O_CC, O_CH, _ = _O

_CONTRACT_LAST = (((1,), (1,)), ((), ()))
_CONTRACT_FIRST = (((0,), (0,)), ((), ()))


def _const_spec(shape):
    zeros = (0,) * len(shape)
    return pl.BlockSpec(shape, lambda *_: zeros, pipeline_mode=pl.Buffered(1))


def _layer_norm(v, g, b):
    mu = jnp.mean(v, axis=-1, keepdims=True)
    d = v - mu
    var = jnp.mean(d * d, axis=-1, keepdims=True)
    return d * lax.rsqrt(var + LN_EPS) * g + b


def _causal_conv3(buf, cols, cur, w_ref, rows):
    buf[SUBLANES:SUBLANES + rows, cols] = cur
    y = buf[SUBLANES - 2:SUBLANES - 2 + rows, cols] * w_ref[0:1, cols]
    y = y + buf[SUBLANES - 1:SUBLANES - 1 + rows, cols] * w_ref[1:2, cols]
    y = y + cur * w_ref[2:3, cols]
    buf[0:SUBLANES, cols] = cur[rows - SUBLANES:rows]
    return y


def _inproj_kernel(x_ref, w_ref, cos_ref, sin_ref, zeta_ref, cw_ref,
                   qr_o, kr_o, kz_o, vr_o, gr_o, mq_o, mk_o, mv_o, km_o, yc_o, pbuf):
    rows = x_ref.shape[0]
    xb = x_ref[...].astype(BF16)

    def proj(c0, width):
        return jnp.dot(xb, w_ref[:, c0:c0 + width], preferred_element_type=F32)

    cos = cos_ref[...]
    sin = sin_ref[...]
    lane = lax.broadcasted_iota(jnp.int32, (rows, LANES), 1)
    first_half = (lane & (HEAD_DIM // 2)) == 0

    def rotary_slab(u):
        partner = jnp.where(first_half,
                            pltpu.roll(u, LANES - HEAD_DIM // 2, 1),
                            pltpu.roll(u, HEAD_DIM // 2, 1))
        return u * cos + partner * sin

    for s in range(N_SLABS):
        cs = slice(s * LANES, (s + 1) * LANES)
        q = rotary_slab(proj(O_RQ + s * LANES, LANES))
        qr_o[:, cs] = q.astype(BF16)
        k = rotary_slab(proj(O_RK + s * LANES, LANES)) * QK_SCALE
        kr_o[:, cs] = k.astype(BF16)
        kz_o[:, cs] = (k * zeta_ref[:, cs]).astype(BF16)

    vr_o[...] = proj(O_RV, RET_WIDTH).astype(BF16)
    rg = proj(O_RG, RET_WIDTH)
    gr_o[...] = rg * jax.nn.sigmoid(rg)
    mq_o[...] = proj(O_MQ, MOBA_WIDTH)
    mk = proj(O_MK, MOBA_WIDTH)
    mk_o[...] = mk.astype(BF16)
    km_o[...] = jnp.mean(mk.reshape(rows // MOBA_BLOCK, MOBA_BLOCK, MOBA_WIDTH), axis=1)
    mv_o[...] = proj(O_MV, MOBA_WIDTH).astype(BF16)

    @pl.when(pl.program_id(1) == 0)
    def _():
        pbuf[0:SUBLANES, :] = jnp.zeros((SUBLANES, CONV_WIDTH), F32)

    cb = proj(O_CB, CONV_WIDTH)
    p = proj(O_CC, CONV_WIDTH) * proj(O_CH, CONV_WIDTH)
    y = _causal_conv3(pbuf, slice(0, CONV_WIDTH), p, cw_ref, rows)
    yc_o[...] = (cb * y).astype(BF16)


def _inproj(x, w_in, cos, sin, zeta, conv_w):
    B, S, _ = x.shape
    nt = S // ROW_TILE
    nblk = ROW_TILE // MOBA_BLOCK
    row = lambda w: pl.BlockSpec((None, ROW_TILE, w), lambda b, t: (b, t, 0))
    sds = jax.ShapeDtypeStruct
    return pl.pallas_call(
        _inproj_kernel,
        grid=(B, nt),
        in_specs=[
            row(D_MODEL),
            _const_spec((D_MODEL, IN_COLS)),
            pl.BlockSpec((ROW_TILE, LANES), lambda b, t: (t, 0)),
            pl.BlockSpec((ROW_TILE, LANES), lambda b, t: (t, 0)),
            _const_spec((ROW_TILE, RET_WIDTH)),
            _const_spec((CONV_K, CONV_WIDTH)),
        ],
        out_specs=[
            row(RET_WIDTH), row(RET_WIDTH), row(RET_WIDTH), row(RET_WIDTH), row(RET_WIDTH),
            row(MOBA_WIDTH), row(MOBA_WIDTH), row(MOBA_WIDTH),
            pl.BlockSpec((None, None, nblk, MOBA_WIDTH), lambda b, t: (b, t, 0, 0)),
            row(CONV_WIDTH),
        ],
        out_shape=[
            sds((B, S, RET_WIDTH), BF16), sds((B, S, RET_WIDTH), BF16),
            sds((B, S, RET_WIDTH), BF16), sds((B, S, RET_WIDTH), BF16),
            sds((B, S, RET_WIDTH), F32),
            sds((B, S, MOBA_WIDTH), F32), sds((B, S, MOBA_WIDTH), BF16),
            sds((B, S, MOBA_WIDTH), BF16),
            sds((B, nt, nblk, MOBA_WIDTH), F32),
            sds((B, S, CONV_WIDTH), BF16),
        ],
        scratch_shapes=[pltpu.VMEM((ROW_TILE + SUBLANES, CONV_WIDTH), F32)],
        compiler_params=pltpu.CompilerParams(
            dimension_semantics=("arbitrary", "arbitrary"), vmem_limit_bytes=VMEM_LIMIT),
        name="inproj",
    )(x, w_in, cos, sin, zeta, conv_w)


def _retention_kernel(q_ref, k_ref, kz_ref, v_ref, g_ref, dec_ref, xi_ref, gc_ref, o_ref):
    n_chunks = q_ref.shape[0] // RET_CHUNK
    for hh in range(HEADS_PER_SLAB):
        hs = slice(hh * HEAD_DIM, (hh + 1) * HEAD_DIM)

        def body(c, state, hh=hh, hs=hs):
            rs = pl.ds(pl.multiple_of(c * RET_CHUNK, RET_CHUNK), RET_CHUNK)
            q = q_ref[rs, hs]
            k = k_ref[rs, hs]
            v = v_ref[rs, hs]
            scores = lax.dot_general(q, k, _CONTRACT_LAST, preferred_element_type=F32) * dec_ref[hh]
            inner = jnp.dot(scores.astype(BF16), v, preferred_element_type=F32)
            cross = jnp.dot(q, state.astype(BF16), preferred_element_type=F32) * xi_ref[hh]
            o = inner + cross
            new_state = state * gc_ref[hh] + lax.dot_general(
                kz_ref[rs, hs], v, _CONTRACT_FIRST, preferred_element_type=F32)
            mu = jnp.mean(o, axis=-1, keepdims=True)
            d = o - mu
            var = jnp.mean(d * d, axis=-1, keepdims=True)
            y = d * lax.rsqrt(var + LN_EPS) * g_ref[rs, hs]
            o_ref[rs, hs] = y.astype(BF16)
            return new_state

        lax.fori_loop(0, n_chunks, body, jnp.zeros((HEAD_DIM, HEAD_DIM), F32))


def _retention(qr, kr, kz, vr, gr, decay, xi, gc):
    B, S, _ = qr.shape
    slab = pl.BlockSpec((None, S, LANES), lambda b, j: (b, 0, j))
    pair = lambda r, c: pl.BlockSpec((HEADS_PER_SLAB, r, c), lambda b, j: (j, 0, 0))
    return pl.pallas_call(
        _retention_kernel,
        grid=(B, N_SLABS),
        in_specs=[slab, slab, slab, slab, slab,
                  pair(RET_CHUNK, RET_CHUNK), pair(RET_CHUNK, HEAD_DIM), pair(HEAD_DIM, HEAD_DIM)],
        out_specs=slab,
        out_shape=jax.ShapeDtypeStruct((B, S, RET_WIDTH), BF16),
        compiler_params=pltpu.CompilerParams(
            dimension_semantics=("arbitrary", "arbitrary"), vmem_limit_bytes=VMEM_LIMIT),
        name="retention",
    )(qr, kr, kz, vr, gr, decay, xi, gc)


def _moba_kernel(q_ref, k_ref, v_ref, km_ref, bias_ref, o_ref, sbuf):
    L = MOBA_BLOCK
    nb = q_ref.shape[0] // L
    row = lax.broadcasted_iota(jnp.int32, (L, L), 0)
    col = lax.broadcasted_iota(jnp.int32, (L, L), 1)
    causal = row >= col
    blk = lax.broadcasted_iota(jnp.int32, (L, nb), 1)
    neg_inf = jnp.float32(-jnp.inf)

    for hh in range(HEADS_PER_SLAB):
        hs = slice(hh * HEAD_DIM, (hh + 1) * HEAD_DIM)
        for j in range(nb):
            q = q_ref[j * L:(j + 1) * L, hs]
            qb = q.astype(BF16)
            sel = None
            if j > MOBA_TOPK:
                gate = lax.dot_general(q, km_ref[:, hs], _CONTRACT_LAST,
                                       precision=lax.Precision.HIGHEST,
                                       preferred_element_type=F32)
                past = blk < j
                sel = []
                for n in range(j):
                    gn = gate[:, n:n + 1]
                    beats = (gate > gn) | ((gate == gn) & (blk < n))
                    rank = jnp.sum(jnp.where(past & beats, 1.0, 0.0), axis=-1, keepdims=True)
                    sel.append(rank < MOBA_TOPK)
            for i in range(j + 1):
                k = k_ref[i * L:(i + 1) * L, hs]
                s = lax.dot_general(qb, k, _CONTRACT_LAST, preferred_element_type=F32)
                s = s * QK_SCALE + bias_ref[hh, min(j - i, 2)]
                if i == j:
                    s = jnp.where(causal, s, neg_inf)
                elif sel is not None:
                    s = jnp.where(sel[i], s, neg_inf)
                sbuf[:, i * L:(i + 1) * L] = s
            w = (j + 1) * L
            sv = sbuf[:, 0:w]
            m = jnp.max(sv, axis=-1, keepdims=True)
            p = jnp.exp(sv - m)
            denom = jnp.sum(p, axis=-1, keepdims=True)
            o = jnp.dot(p.astype(BF16), v_ref[0:w, hs], preferred_element_type=F32) / denom
            o_ref[j * L:(j + 1) * L, hs] = o.astype(BF16)


def _moba(mq, mk, mv, kmean, bias):
    B, S, _ = mq.shape
    nb = S // MOBA_BLOCK
    slab = pl.BlockSpec((None, S, LANES), lambda b, j: (b, 0, j))
    return pl.pallas_call(
        _moba_kernel,
        grid=(B, N_SLABS),
        in_specs=[slab, slab, slab,
                  pl.BlockSpec((None, nb, LANES), lambda b, j: (b, 0, j)),
                  pl.BlockSpec((HEADS_PER_SLAB, 3, MOBA_BLOCK, MOBA_BLOCK), lambda b, j: (j, 0, 0, 0))],
        out_specs=slab,
        out_shape=jax.ShapeDtypeStruct((B, S, MOBA_WIDTH), BF16),
        scratch_shapes=[pltpu.VMEM((MOBA_BLOCK, S), F32)],
        compiler_params=pltpu.CompilerParams(
            dimension_semantics=("arbitrary", "arbitrary"), vmem_limit_bytes=VMEM_LIMIT),
        name="moba",
    )(mq, mk, mv, kmean, bias)


def _mix_ffn_kernel(yr_ref, ym_ref, yc_ref, x_ref, wo_ref, g1_ref, b1_ref,
                    wu_ref, cw_ref, wd_ref, g2_ref, b2_ref, o_ref,
                    abuf, x1_buf, x1b_buf, acc):
    rows = x_ref.shape[0]
    mix = jnp.dot(yr_ref[...], wo_ref[0:RET_WIDTH, :], preferred_element_type=F32)
    mix = mix + jnp.dot(ym_ref[...], wo_ref[RET_WIDTH:RET_WIDTH + MOBA_WIDTH, :],
                        preferred_element_type=F32)
    mix = mix + jnp.dot(yc_ref[...], wo_ref[RET_WIDTH + MOBA_WIDTH:D_MODEL, :],
                        preferred_element_type=F32)
    x1 = _layer_norm(DN_ALPHA * x_ref[...] + mix, g1_ref[...], b1_ref[...])
    x1_buf[...] = x1
    x1b_buf[...] = x1.astype(BF16)

    @pl.when(pl.program_id(1) == 0)
    def _():
        abuf[0:SUBLANES, :] = jnp.zeros((SUBLANES, D_FF), F32)

    for c in range(D_FF // FF_CHUNK):
        cols = slice(c * FF_CHUNK, (c + 1) * FF_CHUNK)
        x1b = x1b_buf[...]
        a = jnp.dot(x1b, wu_ref[:, c * FF_CHUNK:(c + 1) * FF_CHUNK], preferred_element_type=F32)
        gate = jnp.dot(x1b, wu_ref[:, D_FF + c * FF_CHUNK:D_FF + (c + 1) * FF_CHUNK],
                       preferred_element_type=F32)
        a = _causal_conv3(abuf, cols, a, cw_ref, rows)
        h = 0.5 * a * (1.0 + lax.erf(a * math.sqrt(0.5))) * gate
        contrib = jnp.dot(h.astype(BF16), wd_ref[cols, :], preferred_element_type=F32)
        if c == 0:
            acc[...] = contrib
        else:
            acc[...] += contrib

    o_ref[...] = _layer_norm(DN_ALPHA * x1_buf[...] + acc[...], g2_ref[...], b2_ref[...])


def _mix_ffn(y_ret, y_moba, y_conv, x, w_out, g1, b1, w_up, ffn_cw, w_down, g2, b2):
    B, S, _ = x.shape
    row = lambda w: pl.BlockSpec((None, ROW_TILE, w), lambda b, t: (b, t, 0))
    return pl.pallas_call(
        _mix_ffn_kernel,
        grid=(B, S // ROW_TILE),
        in_specs=[
            row(RET_WIDTH), row(MOBA_WIDTH), row(CONV_WIDTH), row(D_MODEL),
            _const_spec((D_MODEL, D_MODEL)), _const_spec((1, D_MODEL)), _const_spec((1, D_MODEL)),
            _const_spec((D_MODEL, 2 * D_FF)), _const_spec((CONV_K, D_FF)),
            _const_spec((D_FF, D_MODEL)), _const_spec((1, D_MODEL)), _const_spec((1, D_MODEL)),
        ],
        out_specs=row(D_MODEL),
        out_shape=jax.ShapeDtypeStruct((B, S, D_MODEL), F32),
        scratch_shapes=[
            pltpu.VMEM((ROW_TILE + SUBLANES, D_FF), F32),
            pltpu.VMEM((ROW_TILE, D_MODEL), F32),
            pltpu.VMEM((ROW_TILE, D_MODEL), BF16),
            pltpu.VMEM((ROW_TILE, D_MODEL), F32),
        ],
        compiler_params=pltpu.CompilerParams(
            dimension_semantics=("arbitrary", "arbitrary"), vmem_limit_bytes=VMEM_LIMIT),
        name="mix_ffn",
    )(y_ret, y_moba, y_conv, x, w_out, g1, b1, w_up, ffn_cw, w_down, g2, b2)


def _t5_bucket(dist):
    n = jnp.maximum(dist, 0)
    max_exact = REL_BUCKETS // 2
    nf = jnp.maximum(n, 1).astype(F32)
    large = max_exact + (jnp.log(nf / max_exact) / math.log(REL_MAX_DIST / max_exact)
                         * (REL_BUCKETS - max_exact)).astype(jnp.int32)
    large = jnp.minimum(large, REL_BUCKETS - 1)
    return jnp.where(n < max_exact, n, large)


def _rotary_tables(S):
    inv = ROPE_BASE ** (-jnp.arange(0, HEAD_DIM, 2, dtype=F32) / HEAD_DIM)
    ang = jnp.arange(S, dtype=F32)[:, None] * inv[None, :]
    cos, sin = jnp.cos(ang), jnp.sin(ang)
    cos_t = jnp.tile(cos, (1, LANES // (HEAD_DIM // 2)))
    sin_t = jnp.tile(jnp.concatenate([-sin, sin], axis=-1), (1, HEADS_PER_SLAB))
    return cos_t, sin_t


def _retention_tables():
    C = RET_CHUNK
    log_g = jnp.log(1.0 - 2.0 ** (-5.0 - jnp.arange(RET_HEADS, dtype=F32)))
    idx = jnp.arange(C, dtype=F32)
    diff = idx[:, None] - idx[None, :]
    decay = jnp.where(diff >= 0, jnp.exp(jnp.maximum(diff, 0.0)[None] * log_g[:, None, None]), 0.0)
    xi = jnp.exp((idx + 1.0)[None, :] * log_g[:, None])
    zeta = jnp.exp((C - 1.0 - idx)[None, :] * log_g[:, None])
    g_chunk = jnp.exp(C * log_g)
    xi_t = jnp.broadcast_to(xi[:, :, None], (RET_HEADS, C, HEAD_DIM))
    zeta_t = jnp.repeat(zeta.T, HEAD_DIM, axis=1)
    zeta_t = jnp.tile(zeta_t, (ROW_TILE // C, 1))
    gc_t = jnp.broadcast_to(g_chunk[:, None, None], (RET_HEADS, HEAD_DIM, HEAD_DIM))
    return decay, xi_t, zeta_t, gc_t


def _bias_tables(rel_bias):
    L = MOBA_BLOCK
    rel_t = rel_bias.T.astype(F32)
    pos = jnp.arange(L, dtype=jnp.int32)
    d_own = pos[:, None] - pos[None, :]
    tiles = [rel_t[:, _t5_bucket(d_own + off)] for off in (0, L, 2 * L)]
    return jnp.stack(tiles, axis=1)


def kernel(x, w_in, conv_w, w_out, ln1_g, ln1_b, w_up, ffn_conv_w, w_down, ln2_g, ln2_b, rel_bias):
    B, S, _ = x.shape
    cos_t, sin_t = _rotary_tables(S)
    decay, xi_t, zeta_t, gc_t = _retention_tables()
    bias = _bias_tables(rel_bias)
    for l in range(DEPTH):
        (qr, kr, kz, vr, gr, mq, mk, mv, km, yc) = _inproj(
            x, w_in[l].astype(BF16), cos_t, sin_t, zeta_t, conv_w[l])
        y_ret = _retention(qr, kr, kz, vr, gr, decay, xi_t, gc_t)
        y_moba = _moba(mq, mk, mv, km.reshape(B, S // MOBA_BLOCK, MOBA_WIDTH), bias)
        x = _mix_ffn(y_ret, y_moba, yc, x, w_out[l].astype(BF16),
                     ln1_g[l][None], ln1_b[l][None], w_up[l].astype(BF16), ffn_conv_w[l],
                     w_down[l].astype(BF16), ln2_g[l][None], ln2_b[l][None])
    return x
```

```python
import math

import numpy as np
import jax
import jax.numpy as jnp
from jax import lax
from jax.experimental import pallas as pl
from jax.experimental.pallas import tpu as pltpu

F32 = jnp.float32
BF16 = jnp.bfloat16

D_MODEL = 1024
HEAD_DIM = 64
RET_HEADS = 6
MOBA_HEADS = 6
RET_WIDTH = RET_HEADS * HEAD_DIM
MOBA_WIDTH = MOBA_HEADS * HEAD_DIM
CONV_WIDTH = 256
IN_COLS = 4 * RET_WIDTH + 3 * MOBA_WIDTH + 3 * CONV_WIDTH
CONV_K = 3
MOBA_BLOCK = 256
MOBA_TOPK = 3
REL_BUCKETS = 32
REL_MAX_DIST = 128
ROPE_BASE = 10000.0
D_FF = 2816
DEPTH = 2
DN_ALPHA = (2.0 * DEPTH) ** 0.25
LN_EPS = 1e-5
QK_SCALE = HEAD_DIM ** -0.5

LANES = 128
SUBLANES = 8
HEADS_PER_SLAB = LANES // HEAD_DIM
N_SLABS = RET_WIDTH // LANES
ROW_TILE = 512
RET_CHUNK = 256
FF_CHUNK = 256
VMEM_LIMIT = 56 * 1024 * 1024
MASK_PENALTY = -1e30
MAX_BLOCKS = 8

_O = [0]
for _w in (RET_WIDTH,) * 4 + (MOBA_WIDTH,) * 3 + (CONV_WIDTH,) * 3:
    _O.append(_O[-1] + _w)
O_RQ, O_RK, O_RV, O_RG, O_MQ, O_MK, O_MV, O_CB, O_CC, O_CH, _ = _O

_CONTRACT_LAST = (((1,), (1,)), ((), ()))
_CONTRACT_FIRST = (((0,), (0,)), ((), ()))


def _const_spec(shape):
    zeros = (0,) * len(shape)
    return pl.BlockSpec(shape, lambda *_: zeros, pipeline_mode=pl.Buffered(1))


def _layer_norm(v, g, b):
    mu = jnp.mean(v, axis=-1, keepdims=True)
    d = v - mu
    var = jnp.mean(d * d, axis=-1, keepdims=True)
    return d * lax.rsqrt(var + LN_EPS) * g + b


def _causal_conv3(buf, cols, cur, w_ref, rows):
    buf[SUBLANES:SUBLANES + rows, cols] = cur
    y = buf[SUBLANES - 2:SUBLANES - 2 + rows, cols] * w_ref[0:1, cols]
    y = y + buf[SUBLANES - 1:SUBLANES - 1 + rows, cols] * w_ref[1:2, cols]
    y = y + cur * w_ref[2:3, cols]
    buf[0:SUBLANES, cols] = cur[rows - SUBLANES:rows]
    return y


def _first_head_lanes(rows):
    return lax.broadcasted_iota(jnp.int32, (rows, LANES), 1) < HEAD_DIM


def _inproj_kernel(x_ref, w_ref, cos_ref, sin_ref, zeta_ref, cw_ref,
                   qr_o, kr_o, kz_o, vr_o, gr_o, mq_o, mk_o, mv_o, km_o, yc_o, pbuf):
    rows = x_ref.shape[0]
    xb = x_ref[...].astype(BF16)

    def proj(c0, width):
        return jnp.dot(xb, w_ref[:, c0:c0 + width], preferred_element_type=F32)

    cos = cos_ref[...]
    sin = sin_ref[...]
    lane = lax.broadcasted_iota(jnp.int32, (rows, LANES), 1)
    first_half = (lane & (HEAD_DIM // 2)) == 0

    def rotary_slab(u):
        partner = jnp.where(first_half,
                            pltpu.roll(u, LANES - HEAD_DIM // 2, 1),
                            pltpu.roll(u, HEAD_DIM // 2, 1))
        return u * cos + partner * sin

    for s in range(N_SLABS):
        cs = slice(s * LANES, (s + 1) * LANES)
        q = rotary_slab(proj(O_RQ + s * LANES, LANES))
        qr_o[:, cs] = q.astype(BF16)
        k = rotary_slab(proj(O_RK + s * LANES, LANES)) * QK_SCALE
        kr_o[:, cs] = k.astype(BF16)
        kz_o[:, cs] = (k * zeta_ref[:, cs]).astype(BF16)

    vr_o[...] = proj(O_RV, RET_WIDTH).astype(BF16)
    rg = proj(O_RG, RET_WIDTH)
    gr_o[...] = rg * jax.nn.sigmoid(rg)
    mq_o[...] = (proj(O_MQ, MOBA_WIDTH) * QK_SCALE).astype(BF16)
    mk = proj(O_MK, MOBA_WIDTH)
    mk_o[...] = mk.astype(BF16)
    km_o[...] = jnp.mean(mk.reshape(rows // MOBA_BLOCK, MOBA_BLOCK, MOBA_WIDTH), axis=1)
    mv_o[...] = proj(O_MV, MOBA_WIDTH).astype(BF16)

    @pl.when(pl.program_id(1) == 0)
    def _():
        pbuf[0:SUBLANES, :] = jnp.zeros((SUBLANES, CONV_WIDTH), F32)

    cb = proj(O_CB, CONV_WIDTH)
    p = proj(O_CC, CONV_WIDTH) * proj(O_CH, CONV_WIDTH)
    y = _causal_conv3(pbuf, slice(0, CONV_WIDTH), p, cw_ref, rows)
    yc_o[...] = (cb * y).astype(BF16)


def _inproj(x, w_in, cos, sin, zeta, conv_w):
    B, S, _ = x.shape
    nt = S // ROW_TILE
    nblk = ROW_TILE // MOBA_BLOCK
    row = lambda w: pl.BlockSpec((None, ROW_TILE, w), lambda b, t: (b, t, 0))
    sds = jax.ShapeDtypeStruct
    return pl.pallas_call(
        _inproj_kernel,
        grid=(B, nt),
        in_specs=[
            row(D_MODEL),
            _const_spec((D_MODEL, IN_COLS)),
            pl.BlockSpec((ROW_TILE, LANES), lambda b, t: (t, 0)),
            pl.BlockSpec((ROW_TILE, LANES), lambda b, t: (t, 0)),
            _const_spec((ROW_TILE, RET_WIDTH)),
            _const_spec((CONV_K, CONV_WIDTH)),
        ],
        out_specs=[
            row(RET_WIDTH), row(RET_WIDTH), row(RET_WIDTH), row(RET_WIDTH), row(RET_WIDTH),
            row(MOBA_WIDTH), row(MOBA_WIDTH), row(MOBA_WIDTH),
            pl.BlockSpec((None, None, nblk, MOBA_WIDTH), lambda b, t: (b, t, 0, 0)),
            row(CONV_WIDTH),
        ],
        out_shape=[
            sds((B, S, RET_WIDTH), BF16), sds((B, S, RET_WIDTH), BF16),
            sds((B, S, RET_WIDTH), BF16), sds((B, S, RET_WIDTH), BF16),
            sds((B, S, RET_WIDTH), F32),
            sds((B, S, MOBA_WIDTH), BF16), sds((B, S, MOBA_WIDTH), BF16),
            sds((B, S, MOBA_WIDTH), BF16),
            sds((B, nt, nblk, MOBA_WIDTH), F32),
            sds((B, S, CONV_WIDTH), BF16),
        ],
        scratch_shapes=[pltpu.VMEM((ROW_TILE + SUBLANES, CONV_WIDTH), F32)],
        compiler_params=pltpu.CompilerParams(
            dimension_semantics=("arbitrary", "arbitrary"), vmem_limit_bytes=VMEM_LIMIT),
        name="inproj",
    )(x, w_in, cos, sin, zeta, conv_w)


def _retention_kernel(q_ref, k_ref, kz_ref, v_ref, g_ref, dec_ref, xi_ref, gc_ref, o_ref):
    C = RET_CHUNK
    n_chunks = q_ref.shape[0] // C
    head0 = _first_head_lanes(C)
    r = lax.broadcasted_iota(jnp.int32, (LANES, LANES), 0) < HEAD_DIM
    c = lax.broadcasted_iota(jnp.int32, (LANES, LANES), 1) < HEAD_DIM
    same_head = r == c
    inv_dim = 1.0 / HEAD_DIM

    def head_mean(t):
        s0 = jnp.sum(jnp.where(head0, t, 0.0), axis=-1, keepdims=True)
        s1 = jnp.sum(jnp.where(head0, 0.0, t), axis=-1, keepdims=True)
        return jnp.where(head0, s0, s1) * inv_dim

    state = jnp.zeros((LANES, LANES), F32)
    for ci in range(n_chunks):
        rs = slice(ci * C, (ci + 1) * C)
        q = q_ref[rs, :]
        k = k_ref[rs, :]
        v = v_ref[rs, :]
        inner = []
        for hh in range(HEADS_PER_SLAB):
            kh = jnp.where(head0 if hh == 0 else jnp.logical_not(head0), k, jnp.zeros_like(k))
            scores = lax.dot_general(q, kh, _CONTRACT_LAST, preferred_element_type=F32) * dec_ref[hh]
            inner.append(jnp.dot(scores.astype(BF16), v, preferred_element_type=F32))
        inner = jnp.where(head0, inner[0], inner[1])
        cross = jnp.dot(q, state.astype(BF16), preferred_element_type=F32) * xi_ref[...]
        o = inner + cross
        kv = lax.dot_general(kz_ref[rs, :], v, _CONTRACT_FIRST, preferred_element_type=F32)
        state = state * gc_ref[...] + jnp.where(same_head, kv, 0.0)
        d = o - head_mean(o)
        var = head_mean(d * d)
        y = d * lax.rsqrt(var + LN_EPS) * g_ref[rs, :]
        o_ref[rs, :] = y.astype(BF16)


def _retention(qr, kr, kz, vr, gr, decay, xi, gc):
    B, S, _ = qr.shape
    slab = pl.BlockSpec((None, S, LANES), lambda b, j: (b, 0, j))
    return pl.pallas_call(
        _retention_kernel,
        grid=(B, N_SLABS),
        in_specs=[slab, slab, slab, slab, slab,
                  pl.BlockSpec((HEADS_PER_SLAB, RET_CHUNK, RET_CHUNK), lambda b, j: (j, 0, 0)),
                  pl.BlockSpec((None, RET_CHUNK, LANES), lambda b, j: (j, 0, 0)),
                  pl.BlockSpec((None, LANES, LANES), lambda b, j: (j, 0, 0))],
        out_specs=slab,
        out_shape=jax.ShapeDtypeStruct((B, S, RET_WIDTH), BF16),
        compiler_params=pltpu.CompilerParams(
            dimension_semantics=("arbitrary", "arbitrary"), vmem_limit_bytes=VMEM_LIMIT),
        name="retention",
    )(qr, kr, kz, vr, gr, decay, xi, gc)


def _aug_lane(p, n):
    return (HEAD_DIM if p == 0 else 0) + n


def _moba_kernel(q_ref, k_ref, v_ref, km_ref, bias_ref, oh_ref, rk_ref, o_ref, qaug, kaug, sbuf):
    L = MOBA_BLOCK
    nb = q_ref.shape[0] // L
    half = L // 2
    head0 = _first_head_lanes(L)
    lane = lax.broadcasted_iota(jnp.int32, (L, LANES), 1)
    pair_m = lane >> 3
    pair_n = lane & 7
    m_before_n = pair_m < pair_n

    for p in range(HEADS_PER_SLAB):
        is_data = head0 if p == 0 else jnp.logical_not(head0)
        for i in range(nb):
            rs = slice(i * L, (i + 1) * L)
            kaug[p, rs, :] = jnp.where(is_data, k_ref[rs, :], oh_ref[p, rs, :])
            qa = q_ref[rs, :]
            if i > MOBA_TOPK:
                g = jnp.dot(qa, km_ref[p], preferred_element_type=F32)
                gm = g[:, :LANES]
                gn = g[:, LANES:]
                valid = (lane < MAX_BLOCKS * MAX_BLOCKS) & (pair_m < i) & (pair_n < i)
                beats = ((gm > gn) | ((gm == gn) & m_before_n)) & valid
                rank = jnp.dot(jnp.where(beats, 1.0, 0.0).astype(BF16), rk_ref[p],
                               preferred_element_type=F32)
                pen = jnp.where(rank >= MOBA_TOPK, MASK_PENALTY, 0.0).astype(BF16)
            else:
                pen = jnp.zeros((L, LANES), BF16)
            qaug[p, rs, :] = jnp.where(is_data, qa, pen)

    for j in range(nb):
        outs = []
        for p in range(HEADS_PER_SLAB):
            qa = qaug[p, j * L:(j + 1) * L, :]
            m_run = None
            for i in range(j + 1):
                s = lax.dot_general(qa, kaug[p, i * L:(i + 1) * L, :], _CONTRACT_LAST,
                                    preferred_element_type=F32)
                if j - i < 2:
                    s = s + bias_ref[p, j - i]
                t = jnp.maximum(s[:, :half], s[:, half:])
                m_run = t if m_run is None else jnp.maximum(m_run, t)
                sbuf[p, :, i * L:(i + 1) * L] = s
            m = jnp.max(m_run, axis=-1, keepdims=True)
            l_run = None
            acc = None
            for i in range(j + 1):
                e = jnp.exp(sbuf[p, :, i * L:(i + 1) * L] - m)
                t = e[:, :half] + e[:, half:]
                l_run = t if l_run is None else l_run + t
                pv = jnp.dot(e.astype(BF16), v_ref[i * L:(i + 1) * L, :], preferred_element_type=F32)
                acc = pv if acc is None else acc + pv
            outs.append(acc / jnp.sum(l_run, axis=-1, keepdims=True))
        o_ref[j * L:(j + 1) * L, :] = jnp.where(head0, outs[0], outs[1]).astype(BF16)


def _moba(mq, mk, mv, kmrep, bias, onehot, rankmat):
    B, S, _ = mq.shape
    slab = pl.BlockSpec((None, S, LANES), lambda b, j: (b, 0, j))
    return pl.pallas_call(
        _moba_kernel,
        grid=(B, N_SLABS),
        in_specs=[slab, slab, slab,
                  pl.BlockSpec((None, HEADS_PER_SLAB, LANES, 2 * LANES), lambda b, j: (b, j, 0, 0)),
                  pl.BlockSpec((HEADS_PER_SLAB, 2, MOBA_BLOCK, MOBA_BLOCK), lambda b, j: (j, 0, 0, 0)),
                  _const_spec((HEADS_PER_SLAB, S, LANES)),
                  _const_spec((HEADS_PER_SLAB, LANES, LANES))],
        out_specs=slab,
        out_shape=jax.ShapeDtypeStruct((B, S, MOBA_WIDTH), BF16),
        scratch_shapes=[pltpu.VMEM((HEADS_PER_SLAB, S, LANES), BF16),
                        pltpu.VMEM((HEADS_PER_SLAB, S, LANES), BF16),
                        pltpu.VMEM((HEADS_PER_SLAB, MOBA_BLOCK, S), F32)],
        compiler_params=pltpu.CompilerParams(
            dimension_semantics=("arbitrary", "arbitrary"), vmem_limit_bytes=VMEM_LIMIT),
        name="moba",
    )(mq, mk, mv, kmrep, bias, onehot, rankmat)


def _mix_ffn_kernel(yr_ref, ym_ref, yc_ref, x_ref, wo_ref, g1_ref, b1_ref,
                    wu_ref, cw_ref, wd_ref, g2_ref, b2_ref, o_ref,
                    abuf, x1_buf, x1b_buf, acc):
    rows = x_ref.shape[0]
    mix = jnp.dot(yr_ref[...], wo_ref[0:RET_WIDTH, :], preferred_element_type=F32)
    mix = mix + jnp.dot(ym_ref[...], wo_ref[RET_WIDTH:RET_WIDTH + MOBA_WIDTH, :],
                        preferred_element_type=F32)
    mix = mix + jnp.dot(yc_ref[...], wo_ref[RET_WIDTH + MOBA_WIDTH:D_MODEL, :],
                        preferred_element_type=F32)
    x1 = _layer_norm(DN_ALPHA * x_ref[...] + mix, g1_ref[...], b1_ref[...])
    x1_buf[...] = x1
    x1b_buf[...] = x1.astype(BF16)

    @pl.when(pl.program_id(1) == 0)
    def _():
        abuf[0:SUBLANES, :] = jnp.zeros((SUBLANES, D_FF), F32)

    for c in range(D_FF // FF_CHUNK):
        cols = slice(c * FF_CHUNK, (c + 1) * FF_CHUNK)
        x1b = x1b_buf[...]
        a = jnp.dot(x1b, wu_ref[:, c * FF_CHUNK:(c + 1) * FF_CHUNK], preferred_element_type=F32)
        gate = jnp.dot(x1b, wu_ref[:, D_FF + c * FF_CHUNK:D_FF + (c + 1) * FF_CHUNK],
                       preferred_element_type=F32)
        a = _causal_conv3(abuf, cols, a, cw_ref, rows)
        h = 0.5 * a * (1.0 + lax.erf(a * math.sqrt(0.5))) * gate
        contrib = jnp.dot(h.astype(BF16), wd_ref[cols, :], preferred_element_type=F32)
        if c == 0:
            acc[...] = contrib
        else:
            acc[...] += contrib

    o_ref[...] = _layer_norm(DN_ALPHA * x1_buf[...] + acc[...], g2_ref[...], b2_ref[...])


def _mix_ffn(y_ret, y_moba, y_conv, x, w_out, g1, b1, w_up, ffn_cw, w_down, g2, b2):
    B, S, _ = x.shape
    row = lambda w: pl.BlockSpec((None, ROW_TILE, w), lambda b, t: (b, t, 0))
    return pl.pallas_call(
        _mix_ffn_kernel,
        grid=(B, S // ROW_TILE),
        in_specs=[
            row(RET_WIDTH), row(MOBA_WIDTH), row(CONV_WIDTH), row(D_MODEL),
            _const_spec((D_MODEL, D_MODEL)), _const_spec((1, D_MODEL)), _const_spec((1, D_MODEL)),
            _const_spec((D_MODEL, 2 * D_FF)), _const_spec((CONV_K, D_FF)),
            _const_spec((D_FF, D_MODEL)), _const_spec((1, D_MODEL)), _const_spec((1, D_MODEL)),
        ],
        out_specs=row(D_MODEL),
        out_shape=jax.ShapeDtypeStruct((B, S, D_MODEL), F32),
        scratch_shapes=[
            pltpu.VMEM((ROW_TILE + SUBLANES, D_FF), F32),
            pltpu.VMEM((ROW_TILE, D_MODEL), F32),
            pltpu.VMEM((ROW_TILE, D_MODEL), BF16),
            pltpu.VMEM((ROW_TILE, D_MODEL), F32),
        ],
        compiler_params=pltpu.CompilerParams(
            dimension_semantics=("arbitrary", "arbitrary"), vmem_limit_bytes=VMEM_LIMIT),
        name="mix_ffn",
    )(y_ret, y_moba, y_conv, x, w_out, g1, b1, w_up, ffn_cw, w_down, g2, b2)


def _t5_bucket(dist):
    n = jnp.maximum(dist, 0)
    max_exact = REL_BUCKETS // 2
    nf = jnp.maximum(n, 1).astype(F32)
    large = max_exact + (jnp.log(nf / max_exact) / math.log(REL_MAX_DIST / max_exact)
                         * (REL_BUCKETS - max_exact)).astype(jnp.int32)
    large = jnp.minimum(large, REL_BUCKETS - 1)
    return jnp.where(n < max_exact, n, large)


def _rotary_tables(S):
    inv = ROPE_BASE ** (-jnp.arange(0, HEAD_DIM, 2, dtype=F32) / HEAD_DIM)
    ang = jnp.arange(S, dtype=F32)[:, None] * inv[None, :]
    cos, sin = jnp.cos(ang), jnp.sin(ang)
    cos_t = jnp.tile(cos, (1, LANES // (HEAD_DIM // 2)))
    sin_t = jnp.tile(jnp.concatenate([-sin, sin], axis=-1), (1, HEADS_PER_SLAB))
    return cos_t, sin_t


def _retention_tables():
    C = RET_CHUNK
    log_g = jnp.log(1.0 - 2.0 ** (-5.0 - jnp.arange(RET_HEADS, dtype=F32)))
    idx = jnp.arange(C, dtype=F32)
    diff = idx[:, None] - idx[None, :]
    decay = jnp.where(diff >= 0, jnp.exp(jnp.maximum(diff, 0.0)[None] * log_g[:, None, None]), 0.0)
    xi = jnp.exp((idx + 1.0)[None, :] * log_g[:, None])
    zeta = jnp.exp((C - 1.0 - idx)[None, :] * log_g[:, None])
    g_chunk = jnp.exp(C * log_g)
    per_lane = lambda t: jnp.repeat(t.T, HEAD_DIM, axis=1)
    xi_t = per_lane(xi).reshape(C, N_SLABS, LANES).transpose(1, 0, 2)
    zeta_t = jnp.tile(per_lane(zeta), (ROW_TILE // C, 1))
    gc_t = jnp.broadcast_to(
        jnp.repeat(g_chunk, HEAD_DIM).reshape(N_SLABS, 1, LANES), (N_SLABS, LANES, LANES))
    return decay, xi_t, zeta_t, gc_t


def _bias_tables(rel_bias):
    L = MOBA_BLOCK
    rel_t = rel_bias.T.astype(F32)
    pos = jnp.arange(L, dtype=jnp.int32)
    d_own = pos[:, None] - pos[None, :]

    def lookup(dist):
        bucket = _t5_bucket(dist)
        out = jnp.zeros((MOBA_HEADS,) + dist.shape, F32)
        for b in range(REL_BUCKETS):
            out = jnp.where(bucket[None] == b, rel_t[:, b].reshape(-1, 1, 1), out)
        return out

    far = lookup(jnp.full((1, 1), 2 * L, jnp.int32))
    own = jnp.where(d_own[None] >= 0, lookup(d_own) - far, -jnp.inf)
    prev = lookup(d_own + L) - far
    return jnp.stack([own, prev], axis=1)


def _moba_tables(S):
    nb = S // MOBA_BLOCK
    assert nb <= MAX_BLOCKS
    onehot = np.zeros((HEADS_PER_SLAB, S, LANES), np.float32)
    rankmat = np.zeros((HEADS_PER_SLAB, LANES, LANES), np.float32)
    for p in range(HEADS_PER_SLAB):
        for r in range(S):
            onehot[p, r, _aug_lane(p, r // MOBA_BLOCK)] = 1.0
        for m in range(MAX_BLOCKS):
            for n in range(MAX_BLOCKS):
                rankmat[p, m * MAX_BLOCKS + n, _aug_lane(p, n)] = 1.0
    return jnp.asarray(onehot, BF16), jnp.asarray(rankmat, BF16)


def _gate_operand(km):
    B, nb, _ = km.shape
    kmt = km.reshape(B, nb, MOBA_HEADS, HEAD_DIM).transpose(0, 2, 3, 1)
    kmt = jnp.pad(kmt, ((0, 0), (0, 0), (0, 0), (0, MAX_BLOCKS - nb)))
    by_m = jnp.repeat(kmt, MAX_BLOCKS, axis=-1)
    by_n = jnp.tile(kmt, (1, 1, 1, MAX_BLOCKS))
    padc = lambda t: jnp.pad(t, ((0, 0), (0, 0), (0, 0), (0, LANES - t.shape[-1])))
    data = jnp.concatenate([padc(by_m), padc(by_n)], axis=-1)
    zeros = jnp.zeros_like(data)
    first = jnp.concatenate([data, zeros], axis=2)
    second = jnp.concatenate([zeros, data], axis=2)
    parity = (jnp.arange(MOBA_HEADS) % HEADS_PER_SLAB).reshape(1, -1, 1, 1)
    return jnp.where(parity == 0, first, second).astype(BF16)


def kernel(x, w_in, conv_w, w_out, ln1_g, ln1_b, w_up, ffn_conv_w, w_down, ln2_g, ln2_b, rel_bias):
    B, S, _ = x.shape
    cos_t, sin_t = _rotary_tables(S)
    decay, xi_t, zeta_t, gc_t = _retention_tables()
    bias = _bias_tables(rel_bias)
    onehot, rankmat = _moba_tables(S)
    for l in range(DEPTH):
        (qr, kr, kz, vr, gr, mq, mk, mv, km, yc) = _inproj(
            x, w_in[l].astype(BF16), cos_t, sin_t, zeta_t, conv_w[l])
        y_ret = _retention(qr, kr, kz, vr, gr, decay, xi_t, gc_t)
        kmrep = _gate_operand(km.reshape(B, S // MOBA_BLOCK, MOBA_WIDTH))
        y_moba = _moba(mq, mk, mv, kmrep, bias, onehot, rankmat)
        x = _mix_ffn(y_ret, y_moba, yc, x, w_out[l].astype(BF16),
                     ln1_g[l][None], ln1_b[l][None], w_up[l].astype(BF16), ffn_conv_w[l],
                     w_down[l].astype(BF16), ln2_g[l][None], ln2_b[l][None])
    return x
```

```python
import math

import numpy as np
import jax
import jax.numpy as jnp
from jax import lax
from jax.experimental import pallas as pl
from jax.experimental.pallas import tpu as pltpu

F32 = jnp.float32
BF16 = jnp.bfloat16

D_MODEL = 1024
HEAD_DIM = 64
RET_HEADS = 6
MOBA_HEADS = 6
RET_WIDTH = RET_HEADS * HEAD_DIM
MOBA_WIDTH = MOBA_HEADS * HEAD_DIM
CONV_WIDTH = 256
IN_COLS = 4 * RET_WIDTH + 3 * MOBA_WIDTH + 3 * CONV_WIDTH
CONV_K = 3
MOBA_BLOCK = 256
MOBA_TOPK = 3
REL_BUCKETS = 32
REL_MAX_DIST = 128
ROPE_BASE = 10000.0
D_FF = 2816
DEPTH = 2
DN_ALPHA = (2.0 * DEPTH) ** 0.25
LN_EPS = 1e-5
QK_SCALE = HEAD_DIM ** -0.5

LANES = 128
SUBLANES = 8
HEADS_PER_SLAB = LANES // HEAD_DIM
N_SLABS = RET_WIDTH // LANES
ROW_TILE = 512
RET_CHUNK = 256
FF_CHUNK = 512
VMEM_LIMIT = 56 * 1024 * 1024
MASK_PENALTY = -1e30
MAX_BLOCKS = 8

_O = [0]
for _w in (RET_WIDTH,) * 4 + (MOBA_WIDTH,) * 3 + (CONV_WIDTH,) * 3:
    _O.append(_O[-1] + _w)
O_RQ, O_RK, O_RV, O_RG, O_MQ, O_MK, O_MV, O_CB, O_CC, O_CH, _ = _O

_CONTRACT_LAST = (((1,), (1,)), ((), ()))
_CONTRACT_FIRST = (((0,), (0,)), ((), ()))


def _const_spec(shape):
    zeros = (0,) * len(shape)
    return pl.BlockSpec(shape, lambda *_: zeros, pipeline_mode=pl.Buffered(1))


def _layer_norm(v, g, b):
    mu = jnp.mean(v, axis=-1, keepdims=True)
    d = v - mu
    var = jnp.mean(d * d, axis=-1, keepdims=True)
    return d * lax.rsqrt(var + LN_EPS) * g + b


def _causal_conv3(buf, cols, cur, w_ref, rows):
    buf[SUBLANES:SUBLANES + rows, cols] = cur
    y = buf[SUBLANES - 2:SUBLANES - 2 + rows, cols] * w_ref[0:1, cols]
    y = y + buf[SUBLANES - 1:SUBLANES - 1 + rows, cols] * w_ref[1:2, cols]
    y = y + cur * w_ref[2:3, cols]
    buf[0:SUBLANES, cols] = cur[rows - SUBLANES:rows]
    return y


def _first_head_lanes(rows):
    return lax.broadcasted_iota(jnp.int32, (rows, LANES), 1) < HEAD_DIM


def _inproj_kernel(x_ref, w_ref, cos_ref, sin_ref, zeta_ref, cw_ref,
                   qr_o, kr_o, kz_o, vr_o, gr_o, mq_o, mk_o, mv_o, km_o, yc_o, pbuf):
    rows = x_ref.shape[0]
    xb = x_ref[...].astype(BF16)

    def proj(c0, width):
        return jnp.dot(xb, w_ref[:, c0:c0 + width], preferred_element_type=F32)

    cos = cos_ref[...]
    sin = sin_ref[...]
    lane = lax.broadcasted_iota(jnp.int32, (rows, LANES), 1)
    first_half = (lane & (HEAD_DIM // 2)) == 0

    def rotary_slab(u):
        partner = jnp.where(first_half,
                            pltpu.roll(u, LANES - HEAD_DIM // 2, 1),
                            pltpu.roll(u, HEAD_DIM // 2, 1))
        return u * cos + partner * sin

    qk = proj(O_RQ, 2 * RET_WIDTH)
    for s in range(N_SLABS):
        cs = slice(s * LANES, (s + 1) * LANES)
        q = rotary_slab(qk[:, cs])
        qr_o[:, cs] = q.astype(BF16)
        k = rotary_slab(qk[:, RET_WIDTH + s * LANES:RET_WIDTH + (s + 1) * LANES]) * QK_SCALE
        kr_o[:, cs] = k.astype(BF16)
        kz_o[:, cs] = (k * zeta_ref[:, cs]).astype(BF16)

    vg = proj(O_RV, 2 * RET_WIDTH)
    vr_o[...] = vg[:, :RET_WIDTH].astype(BF16)
    rg = vg[:, RET_WIDTH:]
    gr_o[...] = rg * jax.nn.sigmoid(rg)
    mqk = proj(O_MQ, 2 * MOBA_WIDTH)
    mq_o[...] = (mqk[:, :MOBA_WIDTH] * QK_SCALE).astype(BF16)
    mk = mqk[:, MOBA_WIDTH:]
    mk_o[...] = mk.astype(BF16)
    km_o[...] = jnp.mean(mk.reshape(rows // MOBA_BLOCK, MOBA_BLOCK, MOBA_WIDTH), axis=1)
    rest = proj(O_MV, MOBA_WIDTH + 3 * CONV_WIDTH)
    mv_o[...] = rest[:, :MOBA_WIDTH].astype(BF16)

    @pl.when(pl.program_id(1) == 0)
    def _():
        pbuf[0:SUBLANES, :] = jnp.zeros((SUBLANES, CONV_WIDTH), F32)

    cb = rest[:, MOBA_WIDTH:MOBA_WIDTH + CONV_WIDTH]
    p = (rest[:, MOBA_WIDTH + CONV_WIDTH:MOBA_WIDTH + 2 * CONV_WIDTH]
         * rest[:, MOBA_WIDTH + 2 * CONV_WIDTH:])
    y = _causal_conv3(pbuf, slice(0, CONV_WIDTH), p, cw_ref, rows)
    yc_o[...] = (cb * y).astype(BF16)


def _inproj(x, w_in, cos, sin, zeta, conv_w):
    B, S, _ = x.shape
    nt = S // ROW_TILE
    nblk = ROW_TILE // MOBA_BLOCK
    row = lambda w: pl.BlockSpec((None, ROW_TILE, w), lambda b, t: (b, t, 0))
    sds = jax.ShapeDtypeStruct
    return pl.pallas_call(
        _inproj_kernel,
        grid=(B, nt),
        in_specs=[
            row(D_MODEL),
            _const_spec((D_MODEL, IN_COLS)),
            pl.BlockSpec((ROW_TILE, LANES), lambda b, t: (t, 0)),
            pl.BlockSpec((ROW_TILE, LANES), lambda b, t: (t, 0)),
            _const_spec((ROW_TILE, RET_WIDTH)),
            _const_spec((CONV_K, CONV_WIDTH)),
        ],
        out_specs=[
            row(RET_WIDTH), row(RET_WIDTH), row(RET_WIDTH), row(RET_WIDTH), row(RET_WIDTH),
            row(MOBA_WIDTH), row(MOBA_WIDTH), row(MOBA_WIDTH),
            pl.BlockSpec((None, None, nblk, MOBA_WIDTH), lambda b, t: (b, t, 0, 0)),
            row(CONV_WIDTH),
        ],
        out_shape=[
            sds((B, S, RET_WIDTH), BF16), sds((B, S, RET_WIDTH), BF16),
            sds((B, S, RET_WIDTH), BF16), sds((B, S, RET_WIDTH), BF16),
            sds((B, S, RET_WIDTH), F32),
            sds((B, S, MOBA_WIDTH), BF16), sds((B, S, MOBA_WIDTH), BF16),
            sds((B, S, MOBA_WIDTH), BF16),
            sds((B, nt, nblk, MOBA_WIDTH), F32),
            sds((B, S, CONV_WIDTH), BF16),
        ],
        scratch_shapes=[pltpu.VMEM((ROW_TILE + SUBLANES, CONV_WIDTH), F32)],
        compiler_params=pltpu.CompilerParams(
            dimension_semantics=("arbitrary", "arbitrary"), vmem_limit_bytes=VMEM_LIMIT),
        name="inproj",
    )(x, w_in, cos, sin, zeta, conv_w)


def _retention_kernel(q_ref, k_ref, kz_ref, v_ref, g_ref, dec_ref, xi_ref, gc_ref, o_ref):
    C = RET_CHUNK
    n_chunks = q_ref.shape[0] // C
    head0 = _first_head_lanes(C)
    r = lax.broadcasted_iota(jnp.int32, (LANES, LANES), 0) < HEAD_DIM
    c = lax.broadcasted_iota(jnp.int32, (LANES, LANES), 1) < HEAD_DIM
    same_head = r == c
    inv_dim = 1.0 / HEAD_DIM

    def head_mean(t):
        s0 = jnp.sum(jnp.where(head0, t, 0.0), axis=-1, keepdims=True)
        s1 = jnp.sum(jnp.where(head0, 0.0, t), axis=-1, keepdims=True)
        return jnp.where(head0, s0, s1) * inv_dim

    state = jnp.zeros((LANES, LANES), F32)
    for ci in range(n_chunks):
        rs = slice(ci * C, (ci + 1) * C)
        q = q_ref[rs, :]
        k = k_ref[rs, :]
        v = v_ref[rs, :]
        inner = []
        for hh in range(HEADS_PER_SLAB):
            kh = jnp.where(head0 if hh == 0 else jnp.logical_not(head0), k, jnp.zeros_like(k))
            scores = lax.dot_general(q, kh, _CONTRACT_LAST, preferred_element_type=F32) * dec_ref[hh]
            inner.append(jnp.dot(scores.astype(BF16), v, preferred_element_type=F32))
        inner = jnp.where(head0, inner[0], inner[1])
        cross = jnp.dot(q, state.astype(BF16), preferred_element_type=F32) * xi_ref[...]
        o = inner + cross
        kv = lax.dot_general(kz_ref[rs, :], v, _CONTRACT_FIRST, preferred_element_type=F32)
        state = state * gc_ref[...] + jnp.where(same_head, kv, 0.0)
        d = o - head_mean(o)
        var = head_mean(d * d)
        y = d * lax.rsqrt(var + LN_EPS) * g_ref[rs, :]
        o_ref[rs, :] = y.astype(BF16)


def _retention(qr, kr, kz, vr, gr, decay, xi, gc):
    B, S, _ = qr.shape
    slab = pl.BlockSpec((None, S, LANES), lambda b, j: (b, 0, j))
    return pl.pallas_call(
        _retention_kernel,
        grid=(B, N_SLABS),
        in_specs=[slab, slab, slab, slab, slab,
                  pl.BlockSpec((HEADS_PER_SLAB, RET_CHUNK, RET_CHUNK), lambda b, j: (j, 0, 0)),
                  pl.BlockSpec((None, RET_CHUNK, LANES), lambda b, j: (j, 0, 0)),
                  pl.BlockSpec((None, LANES, LANES), lambda b, j: (j, 0, 0))],
        out_specs=slab,
        out_shape=jax.ShapeDtypeStruct((B, S, RET_WIDTH), BF16),
        compiler_params=pltpu.CompilerParams(
            dimension_semantics=("arbitrary", "arbitrary"), vmem_limit_bytes=VMEM_LIMIT),
        name="retention",
    )(qr, kr, kz, vr, gr, decay, xi, gc)


def _aug_lane(p, n):
    return (HEAD_DIM if p == 0 else 0) + n


def _moba_kernel(q_ref, k_ref, v_ref, km_ref, bias_ref, oh_ref, rk_ref, o_ref, qaug, kaug, sbuf):
    L = MOBA_BLOCK
    nb = q_ref.shape[0] // L
    half = L // 2
    head0 = _first_head_lanes(L)
    lane = lax.broadcasted_iota(jnp.int32, (L, LANES), 1)
    pair_m = lane >> 3
    pair_n = lane & 7
    m_before_n = pair_m < pair_n

    for p in range(HEADS_PER_SLAB):
        is_data = head0 if p == 0 else jnp.logical_not(head0)
        for i in range(nb):
            rs = slice(i * L, (i + 1) * L)
            kaug[p, rs, :] = jnp.where(is_data, k_ref[rs, :], oh_ref[p, rs, :])
            qa = q_ref[rs, :]
            if i > MOBA_TOPK:
                g = jnp.dot(qa, km_ref[p], preferred_element_type=F32)
                gm = g[:, :LANES]
                gn = g[:, LANES:]
                valid = (lane < MAX_BLOCKS * MAX_BLOCKS) & (pair_m < i) & (pair_n < i)
                beats = ((gm > gn) | ((gm == gn) & m_before_n)) & valid
                rank = jnp.dot(jnp.where(beats, 1.0, 0.0).astype(BF16), rk_ref[p],
                               preferred_element_type=F32)
                pen = jnp.where(rank >= MOBA_TOPK, MASK_PENALTY, 0.0).astype(BF16)
            else:
                pen = jnp.zeros((L, LANES), BF16)
            qaug[p, rs, :] = jnp.where(is_data, qa, pen)

    for j in range(nb):
        outs = []
        for p in range(HEADS_PER_SLAB):
            qa = qaug[p, j * L:(j + 1) * L, :]
            m_run = None
            for i in range(j + 1):
                s = lax.dot_general(qa, kaug[p, i * L:(i + 1) * L, :], _CONTRACT_LAST,
                                    preferred_element_type=F32)
                if j - i < 2:
                    s = s + bias_ref[p, j - i]
                t = jnp.maximum(s[:, :half], s[:, half:])
                m_run = t if m_run is None else jnp.maximum(m_run, t)
                sbuf[p, :, i * L:(i + 1) * L] = s
            m = jnp.max(m_run, axis=-1, keepdims=True)
            l_run = None
            acc = None
            for i in range(j + 1):
                e = jnp.exp(sbuf[p, :, i * L:(i + 1) * L] - m)
                t = e[:, :half] + e[:, half:]
                l_run = t if l_run is None else l_run + t
                pv = jnp.dot(e.astype(BF16), v_ref[i * L:(i + 1) * L, :], preferred_element_type=F32)
                acc = pv if acc is None else acc + pv
            outs.append(acc / jnp.sum(l_run, axis=-1, keepdims=True))
        o_ref[j * L:(j + 1) * L, :] = jnp.where(head0, outs[0], outs[1]).astype(BF16)


def _moba(mq, mk, mv, kmrep, bias, onehot, rankmat):
    B, S, _ = mq.shape
    slab = pl.BlockSpec((None, S, LANES), lambda b, j: (b, 0, j))
    return pl.pallas_call(
        _moba_kernel,
        grid=(B, N_SLABS),
        in_specs=[slab, slab, slab,
                  pl.BlockSpec((None, HEADS_PER_SLAB, LANES, 2 * LANES), lambda b, j: (b, j, 0, 0)),
                  pl.BlockSpec((HEADS_PER_SLAB, 2, MOBA_BLOCK, MOBA_BLOCK), lambda b, j: (j, 0, 0, 0)),
                  _const_spec((HEADS_PER_SLAB, S, LANES)),
                  _const_spec((HEADS_PER_SLAB, LANES, LANES))],
        out_specs=slab,
        out_shape=jax.ShapeDtypeStruct((B, S, MOBA_WIDTH), BF16),
        scratch_shapes=[pltpu.VMEM((HEADS_PER_SLAB, S, LANES), BF16),
                        pltpu.VMEM((HEADS_PER_SLAB, S, LANES), BF16),
                        pltpu.VMEM((HEADS_PER_SLAB, MOBA_BLOCK, S), F32)],
        compiler_params=pltpu.CompilerParams(
            dimension_semantics=("arbitrary", "arbitrary"), vmem_limit_bytes=VMEM_LIMIT),
        name="moba",
    )(mq, mk, mv, kmrep, bias, onehot, rankmat)


def _mix_ffn_kernel(yr_ref, ym_ref, yc_ref, x_ref, wo_ref, g1_ref, b1_ref,
                    wu_ref, cw_ref, wd_ref, g2_ref, b2_ref, o_ref,
                    abuf, x1_buf, x1b_buf, acc):
    rows = x_ref.shape[0]
    cat = jnp.concatenate([yr_ref[...], ym_ref[...], yc_ref[...]], axis=1)
    mix = jnp.dot(cat, wo_ref[...], preferred_element_type=F32)
    x1 = _layer_norm(DN_ALPHA * x_ref[...] + mix, g1_ref[...], b1_ref[...])
    x1_buf[...] = x1
    x1b_buf[...] = x1.astype(BF16)

    @pl.when(pl.program_id(1) == 0)
    def _():
        abuf[0:SUBLANES, :] = jnp.zeros((SUBLANES, D_FF), F32)

    for c0 in range(0, D_FF, FF_CHUNK):
        c1 = min(c0 + FF_CHUNK, D_FF)
        cols = slice(c0, c1)
        x1b = x1b_buf[...]
        a = jnp.dot(x1b, wu_ref[:, c0:c1], preferred_element_type=F32)
        gate = jnp.dot(x1b, wu_ref[:, D_FF + c0:D_FF + c1], preferred_element_type=F32)
        a = _causal_conv3(abuf, cols, a, cw_ref, rows)
        h = 0.5 * a * (1.0 + lax.erf(a * math.sqrt(0.5))) * gate
        contrib = jnp.dot(h.astype(BF16), wd_ref[cols, :], preferred_element_type=F32)
        if c0 == 0:
            acc[...] = contrib
        else:
            acc[...] += contrib

    o_ref[...] = _layer_norm(DN_ALPHA * x1_buf[...] + acc[...], g2_ref[...], b2_ref[...])


def _mix_ffn(y_ret, y_moba, y_conv, x, w_out, g1, b1, w_up, ffn_cw, w_down, g2, b2):
    B, S, _ = x.shape
    row = lambda w: pl.BlockSpec((None, ROW_TILE, w), lambda b, t: (b, t, 0))
    return pl.pallas_call(
        _mix_ffn_kernel,
        grid=(B, S // ROW_TILE),
        in_specs=[
            row(RET_WIDTH), row(MOBA_WIDTH), row(CONV_WIDTH), row(D_MODEL),
            _const_spec((D_MODEL, D_MODEL)), _const_spec((1, D_MODEL)), _const_spec((1, D_MODEL)),
            _const_spec((D_MODEL, 2 * D_FF)), _const_spec((CONV_K, D_FF)),
            _const_spec((D_FF, D_MODEL)), _const_spec((1, D_MODEL)), _const_spec((1, D_MODEL)),
        ],
        out_specs=row(D_MODEL),
        out_shape=jax.ShapeDtypeStruct((B, S, D_MODEL), F32),
        scratch_shapes=[
            pltpu.VMEM((ROW_TILE + SUBLANES, D_FF), F32),
            pltpu.VMEM((ROW_TILE, D_MODEL), F32),
            pltpu.VMEM((ROW_TILE, D_MODEL), BF16),
            pltpu.VMEM((ROW_TILE, D_MODEL), F32),
        ],
        compiler_params=pltpu.CompilerParams(
            dimension_semantics=("arbitrary", "arbitrary"), vmem_limit_bytes=VMEM_LIMIT),
        name="mix_ffn",
    )(y_ret, y_moba, y_conv, x, w_out, g1, b1, w_up, ffn_cw, w_down, g2, b2)


def _t5_bucket(dist):
    n = jnp.maximum(dist, 0)
    max_exact = REL_BUCKETS // 2
    nf = jnp.maximum(n, 1).astype(F32)
    large = max_exact + (jnp.log(nf / max_exact) / math.log(REL_MAX_DIST / max_exact)
                         * (REL_BUCKETS - max_exact)).astype(jnp.int32)
    large = jnp.minimum(large, REL_BUCKETS - 1)
    return jnp.where(n < max_exact, n, large)


def _rotary_tables(S):
    inv = ROPE_BASE ** (-jnp.arange(0, HEAD_DIM, 2, dtype=F32) / HEAD_DIM)
    ang = jnp.arange(S, dtype=F32)[:, None] * inv[None, :]
    cos, sin = jnp.cos(ang), jnp.sin(ang)
    cos_t = jnp.tile(cos, (1, LANES // (HEAD_DIM // 2)))
    sin_t = jnp.tile(jnp.concatenate([-sin, sin], axis=-1), (1, HEADS_PER_SLAB))
    return cos_t, sin_t


def _retention_tables():
    C = RET_CHUNK
    log_g = jnp.log(1.0 - 2.0 ** (-5.0 - jnp.arange(RET_HEADS, dtype=F32)))
    idx = jnp.arange(C, dtype=F32)
    diff = idx[:, None] - idx[None, :]
    decay = jnp.where(diff >= 0, jnp.exp(jnp.maximum(diff, 0.0)[None] * log_g[:, None, None]), 0.0)
    xi = jnp.exp((idx + 1.0)[None, :] * log_g[:, None])
    zeta = jnp.exp((C - 1.0 - idx)[None, :] * log_g[:, None])
    g_chunk = jnp.exp(C * log_g)
    per_lane = lambda t: jnp.repeat(t.T, HEAD_DIM, axis=1)
    xi_t = per_lane(xi).reshape(C, N_SLABS, LANES).transpose(1, 0, 2)
    zeta_t = jnp.tile(per_lane(zeta), (ROW_TILE // C, 1))
    gc_t = jnp.broadcast_to(
        jnp.repeat(g_chunk, HEAD_DIM).reshape(N_SLABS, 1, LANES), (N_SLABS, LANES, LANES))
    return decay, xi_t, zeta_t, gc_t


def _bias_tables(rel_bias):
    L = MOBA_BLOCK
    rel_t = rel_bias.T.astype(F32)
    pos = jnp.arange(L, dtype=jnp.int32)
    d_own = pos[:, None] - pos[None, :]

    def lookup(dist):
        bucket = _t5_bucket(dist)
        out = jnp.zeros((MOBA_HEADS,) + dist.shape, F32)
        for b in range(REL_BUCKETS):
            out = jnp.where(bucket[None] == b, rel_t[:, b].reshape(-1, 1, 1), out)
        return out

    far = lookup(jnp.full((1, 1), 2 * L, jnp.int32))
    own = jnp.where(d_own[None] >= 0, lookup(d_own) - far, -jnp.inf)
    prev = lookup(d_own + L) - far
    return jnp.stack([own, prev], axis=1)


def _moba_tables(S):
    nb = S // MOBA_BLOCK
    assert nb <= MAX_BLOCKS
    onehot = np.zeros((HEADS_PER_SLAB, S, LANES), np.float32)
    rankmat = np.zeros((HEADS_PER_SLAB, LANES, LANES), np.float32)
    for p in range(HEADS_PER_SLAB):
        for r in range(S):
            onehot[p, r, _aug_lane(p, r // MOBA_BLOCK)] = 1.0
        for m in range(MAX_BLOCKS):
            for n in range(MAX_BLOCKS):
                rankmat[p, m * MAX_BLOCKS + n, _aug_lane(p, n)] = 1.0
    return jnp.asarray(onehot, BF16), jnp.asarray(rankmat, BF16)


def _gate_operand(km):
    B, nb, _ = km.shape
    kmt = km.reshape(B, nb, MOBA_HEADS, HEAD_DIM).transpose(0, 2, 3, 1)
    kmt = jnp.pad(kmt, ((0, 0), (0, 0), (0, 0), (0, MAX_BLOCKS - nb)))
    by_m = jnp.repeat(kmt, MAX_BLOCKS, axis=-1)
    by_n = jnp.tile(kmt, (1, 1, 1, MAX_BLOCKS))
    padc = lambda t: jnp.pad(t, ((0, 0), (0, 0), (0, 0), (0, LANES - t.shape[-1])))
    data = jnp.concatenate([padc(by_m), padc(by_n)], axis=-1)
    zeros = jnp.zeros_like(data)
    first = jnp.concatenate([data, zeros], axis=2)
    second = jnp.concatenate([zeros, data], axis=2)
    parity = (jnp.arange(MOBA_HEADS) % HEADS_PER_SLAB).reshape(1, -1, 1, 1)
    return jnp.where(parity == 0, first, second).astype(BF16)


def kernel(x, w_in, conv_w, w_out, ln1_g, ln1_b, w_up, ffn_conv_w, w_down, ln2_g, ln2_b, rel_bias):
    B, S, _ = x.shape
    cos_t, sin_t = _rotary_tables(S)
    decay, xi_t, zeta_t, gc_t = _retention_tables()
    bias = _bias_tables(rel_bias)
    onehot, rankmat = _moba_tables(S)
    for l in range(DEPTH):
        (qr, kr, kz, vr, gr, mq, mk, mv, km, yc) = _inproj(
            x, w_in[l].astype(BF16), cos_t, sin_t, zeta_t, conv_w[l])
        y_ret = _retention(qr, kr, kz, vr, gr, decay, xi_t, gc_t)
        kmrep = _gate_operand(km.reshape(B, S // MOBA_BLOCK, MOBA_WIDTH))
        y_moba = _moba(mq, mk, mv, kmrep, bias, onehot, rankmat)
        x = _mix_ffn(y_ret, y_moba, yc, x, w_out[l].astype(BF16),
                     ln1_g[l][None], ln1_b[l][None], w_up[l].astype(BF16), ffn_conv_w[l],
                     w_down[l].astype(BF16), ln2_g[l][None], ln2_b[l][None])
    return x
```

```python
import math

import numpy as np
import jax
import jax.numpy as jnp
from jax import lax
from jax.experimental import pallas as pl
from jax.experimental.pallas import tpu as pltpu

F32 = jnp.float32
BF16 = jnp.bfloat16

D_MODEL = 1024
HEAD_DIM = 64
RET_HEADS = 6
MOBA_HEADS = 6
RET_WIDTH = RET_HEADS * HEAD_DIM
MOBA_WIDTH = MOBA_HEADS * HEAD_DIM
CONV_WIDTH = 256
IN_COLS = 4 * RET_WIDTH + 3 * MOBA_WIDTH + 3 * CONV_WIDTH
CONV_K = 3
MOBA_BLOCK = 256
MOBA_TOPK = 3
REL_BUCKETS = 32
REL_MAX_DIST = 128
ROPE_BASE = 10000.0
D_FF = 2816
DEPTH = 2
DN_ALPHA = (2.0 * DEPTH) ** 0.25
LN_EPS = 1e-5
QK_SCALE = HEAD_DIM ** -0.5

LANES = 128
SUBLANES = 8
HEADS_PER_SLAB = LANES // HEAD_DIM
N_SLABS = RET_WIDTH // LANES
ROW_TILE = 512
RET_CHUNK = 256
FF_CHUNK = 512
VMEM_LIMIT = 56 * 1024 * 1024
MASK_PENALTY = -1e30
MAX_BLOCKS = 8

_O = [0]
for _w in (RET_WIDTH,) * 4 + (MOBA_WIDTH,) * 3 + (CONV_WIDTH,) * 3:
    _O.append(_O[-1] + _w)
O_RQ, O_RK, O_RV, O_RG, O_MQ, O_MK, O_MV, O_CB, O_CC, O_CH, _ = _O

_CONTRACT_LAST = (((1,), (1,)), ((), ()))
_CONTRACT_FIRST = (((0,), (0,)), ((), ()))


def _const_spec(shape):
    zeros = (0,) * len(shape)
    return pl.BlockSpec(shape, lambda *_: zeros, pipeline_mode=pl.Buffered(1))


def _layer_norm(v, g, b):
    mu = jnp.mean(v, axis=-1, keepdims=True)
    d = v - mu
    var = jnp.mean(d * d, axis=-1, keepdims=True)
    return d * lax.rsqrt(var + LN_EPS) * g + b


def _causal_conv3(buf, cols, cur, w_ref, rows):
    buf[SUBLANES:SUBLANES + rows, cols] = cur
    y = buf[SUBLANES - 2:SUBLANES - 2 + rows, cols] * w_ref[0:1, cols]
    y = y + buf[SUBLANES - 1:SUBLANES - 1 + rows, cols] * w_ref[1:2, cols]
    y = y + cur * w_ref[2:3, cols]
    buf[0:SUBLANES, cols] = cur[rows - SUBLANES:rows]
    return y


def _first_head_lanes(rows):
    return lax.broadcasted_iota(jnp.int32, (rows, LANES), 1) < HEAD_DIM


def _inproj_kernel(x_ref, w_ref, cos_ref, sin_ref, zeta_ref, cw_ref,
                   qr_o, kr_o, kz_o, vr_o, gr_o, mq_o, mk_o, mv_o, km_o, yc_o, pbuf):
    rows = x_ref.shape[0]
    xb = x_ref[...].astype(BF16)

    def proj(c0, width):
        return jnp.dot(xb, w_ref[:, c0:c0 + width], preferred_element_type=F32)

    cos = cos_ref[...]
    sin = sin_ref[...]
    lane = lax.broadcasted_iota(jnp.int32, (rows, LANES), 1)
    first_half = (lane & (HEAD_DIM // 2)) == 0

    def rotary_slab(u):
        partner = jnp.where(first_half,
                            pltpu.roll(u, LANES - HEAD_DIM // 2, 1),
                            pltpu.roll(u, HEAD_DIM // 2, 1))
        return u * cos + partner * sin

    qk = proj(O_RQ, 2 * RET_WIDTH)
    for s in range(N_SLABS):
        cs = slice(s * LANES, (s + 1) * LANES)
        q = rotary_slab(qk[:, cs])
        qr_o[:, cs] = q.astype(BF16)
        k = rotary_slab(qk[:, RET_WIDTH + s * LANES:RET_WIDTH + (s + 1) * LANES]) * QK_SCALE
        kr_o[:, cs] = k.astype(BF16)
        kz_o[:, cs] = (k * zeta_ref[:, cs]).astype(BF16)

    vg = proj(O_RV, 2 * RET_WIDTH)
    vr_o[...] = vg[:, :RET_WIDTH].astype(BF16)
    rg = vg[:, RET_WIDTH:]
    gr_o[...] = rg * jax.nn.sigmoid(rg)
    mqk = proj(O_MQ, 2 * MOBA_WIDTH)
    mq_o[...] = (mqk[:, :MOBA_WIDTH] * QK_SCALE).astype(BF16)
    mk = mqk[:, MOBA_WIDTH:]
    mk_o[...] = mk.astype(BF16)
    km_o[...] = jnp.mean(mk.reshape(rows // MOBA_BLOCK, MOBA_BLOCK, MOBA_WIDTH), axis=1)
    rest = proj(O_MV, MOBA_WIDTH + 3 * CONV_WIDTH)
    mv_o[...] = rest[:, :MOBA_WIDTH].astype(BF16)

    @pl.when(pl.program_id(1) == 0)
    def _():
        pbuf[0:SUBLANES, :] = jnp.zeros((SUBLANES, CONV_WIDTH), F32)

    cb = rest[:, MOBA_WIDTH:MOBA_WIDTH + CONV_WIDTH]
    p = (rest[:, MOBA_WIDTH + CONV_WIDTH:MOBA_WIDTH + 2 * CONV_WIDTH]
         * rest[:, MOBA_WIDTH + 2 * CONV_WIDTH:])
    y = _causal_conv3(pbuf, slice(0, CONV_WIDTH), p, cw_ref, rows)
    yc_o[...] = (cb * y).astype(BF16)


def _inproj(x, w_in, cos, sin, zeta, conv_w):
    B, S, _ = x.shape
    nt = S // ROW_TILE
    nblk = ROW_TILE // MOBA_BLOCK
    row = lambda w: pl.BlockSpec((None, ROW_TILE, w), lambda b, t: (b, t, 0))
    sds = jax.ShapeDtypeStruct
    return pl.pallas_call(
        _inproj_kernel,
        grid=(B, nt),
        in_specs=[
            row(D_MODEL),
            _const_spec((D_MODEL, IN_COLS)),
            pl.BlockSpec((ROW_TILE, LANES), lambda b, t: (t, 0)),
            pl.BlockSpec((ROW_TILE, LANES), lambda b, t: (t, 0)),
            _const_spec((ROW_TILE, RET_WIDTH)),
            _const_spec((CONV_K, CONV_WIDTH)),
        ],
        out_specs=[
            row(RET_WIDTH), row(RET_WIDTH), row(RET_WIDTH), row(RET_WIDTH), row(RET_WIDTH),
            row(MOBA_WIDTH), row(MOBA_WIDTH), row(MOBA_WIDTH),
            pl.BlockSpec((None, None, nblk, MOBA_WIDTH), lambda b, t: (b, t, 0, 0)),
            row(CONV_WIDTH),
        ],
        out_shape=[
            sds((B, S, RET_WIDTH), BF16), sds((B, S, RET_WIDTH), BF16),
            sds((B, S, RET_WIDTH), BF16), sds((B, S, RET_WIDTH), BF16),
            sds((B, S, RET_WIDTH), F32),
            sds((B, S, MOBA_WIDTH), BF16), sds((B, S, MOBA_WIDTH), BF16),
            sds((B, S, MOBA_WIDTH), BF16),
            sds((B, nt, nblk, MOBA_WIDTH), F32),
            sds((B, S, CONV_WIDTH), BF16),
        ],
        scratch_shapes=[pltpu.VMEM((ROW_TILE + SUBLANES, CONV_WIDTH), F32)],
        compiler_params=pltpu.CompilerParams(
            dimension_semantics=("arbitrary", "arbitrary"), vmem_limit_bytes=VMEM_LIMIT),
        name="inproj",
    )(x, w_in, cos, sin, zeta, conv_w)


def _retention_kernel(q_ref, k_ref, kz_ref, v_ref, g_ref, dec_ref, xi_ref, gc_ref, o_ref):
    C = RET_CHUNK
    n_chunks = q_ref.shape[0] // C
    head0 = _first_head_lanes(C)
    r = lax.broadcasted_iota(jnp.int32, (LANES, LANES), 0) < HEAD_DIM
    c = lax.broadcasted_iota(jnp.int32, (LANES, LANES), 1) < HEAD_DIM
    same_head = r == c
    inv_dim = 1.0 / HEAD_DIM

    def head_mean(t):
        s0 = jnp.sum(jnp.where(head0, t, 0.0), axis=-1, keepdims=True)
        s1 = jnp.sum(jnp.where(head0, 0.0, t), axis=-1, keepdims=True)
        return jnp.where(head0, s0, s1) * inv_dim

    state = jnp.zeros((LANES, LANES), F32)
    for ci in range(n_chunks):
        rs = slice(ci * C, (ci + 1) * C)
        q = q_ref[rs, :]
        k = k_ref[rs, :]
        v = v_ref[rs, :]
        inner = []
        for hh in range(HEADS_PER_SLAB):
            kh = jnp.where(head0 if hh == 0 else jnp.logical_not(head0), k, jnp.zeros_like(k))
            scores = lax.dot_general(q, kh, _CONTRACT_LAST, preferred_element_type=F32) * dec_ref[hh]
            inner.append(jnp.dot(scores.astype(BF16), v, preferred_element_type=F32))
        inner = jnp.where(head0, inner[0], inner[1])
        cross = jnp.dot(q, state.astype(BF16), preferred_element_type=F32) * xi_ref[...]
        o = inner + cross
        kv = lax.dot_general(kz_ref[rs, :], v, _CONTRACT_FIRST, preferred_element_type=F32)
        state = state * gc_ref[...] + jnp.where(same_head, kv, 0.0)
        d = o - head_mean(o)
        var = head_mean(d * d)
        y = d * lax.rsqrt(var + LN_EPS) * g_ref[rs, :]
        o_ref[rs, :] = y.astype(BF16)


def _retention(qr, kr, kz, vr, gr, decay, xi, gc):
    B, S, _ = qr.shape
    slab = pl.BlockSpec((None, S, LANES), lambda b, j: (b, 0, j))
    return pl.pallas_call(
        _retention_kernel,
        grid=(B, N_SLABS),
        in_specs=[slab, slab, slab, slab, slab,
                  pl.BlockSpec((HEADS_PER_SLAB, RET_CHUNK, RET_CHUNK), lambda b, j: (j, 0, 0)),
                  pl.BlockSpec((None, RET_CHUNK, LANES), lambda b, j: (j, 0, 0)),
                  pl.BlockSpec((None, LANES, LANES), lambda b, j: (j, 0, 0))],
        out_specs=slab,
        out_shape=jax.ShapeDtypeStruct((B, S, RET_WIDTH), BF16),
        compiler_params=pltpu.CompilerParams(
            dimension_semantics=("arbitrary", "arbitrary"), vmem_limit_bytes=VMEM_LIMIT),
        name="retention",
    )(qr, kr, kz, vr, gr, decay, xi, gc)


def _aug_lane(p, n):
    return (HEAD_DIM if p == 0 else 0) + n


def _moba_kernel(q_ref, k_ref, v_ref, km_ref, bias_ref, oh_ref, rk_ref, o_ref, qaug, kaug, sbuf):
    L = MOBA_BLOCK
    S = q_ref.shape[0]
    nb = S // L
    half = L // 2
    first_ranked = min(MOBA_TOPK + 1, nb)
    head0 = _first_head_lanes(L)
    lane = lax.broadcasted_iota(jnp.int32, (L, LANES), 1)
    pair_m = lane >> 3
    pair_n = lane & 7
    m_before_n = pair_m < pair_n
    blocks = lambda i0, i1: slice(i0 * L, i1 * L)
    tile_row = [0]
    for i in range(nb):
        tile_row.append(tile_row[-1] + (nb - i) * L)

    outs = []
    for p in range(HEADS_PER_SLAB):
        is_data = head0 if p == 0 else jnp.logical_not(head0)

        for i in range(nb):
            kaug[p, blocks(i, i + 1), :] = jnp.where(is_data, k_ref[blocks(i, i + 1), :],
                                                     oh_ref[p, blocks(i, i + 1), :])
        for i in range(first_ranked):
            qaug[p, blocks(i, i + 1), :] = jnp.where(is_data, q_ref[blocks(i, i + 1), :],
                                                     jnp.zeros((L, LANES), BF16))
        if first_ranked < nb:
            g = jnp.dot(q_ref[blocks(first_ranked, nb), :], km_ref[p], preferred_element_type=F32)
            beats = []
            for i in range(first_ranked, nb):
                gi = g[blocks(i - first_ranked, i - first_ranked + 1), :]
                gm = gi[:, :LANES]
                gn = gi[:, LANES:]
                valid = (lane < MAX_BLOCKS * MAX_BLOCKS) & (pair_m < i) & (pair_n < i)
                b = ((gm > gn) | ((gm == gn) & m_before_n)) & valid
                beats.append(jnp.where(b, 1.0, 0.0).astype(BF16))
            rank = jnp.dot(jnp.concatenate(beats, axis=0), rk_ref[p], preferred_element_type=F32)
            for i in range(first_ranked, nb):
                r = rank[blocks(i - first_ranked, i - first_ranked + 1), :]
                pen = jnp.where(r >= MOBA_TOPK, MASK_PENALTY, 0.0).astype(BF16)
                qaug[p, blocks(i, i + 1), :] = jnp.where(is_data, q_ref[blocks(i, i + 1), :], pen)

        m_run = [None] * nb
        for i in range(nb):
            sc = lax.dot_general(qaug[p, blocks(i, nb), :], kaug[p, blocks(i, i + 1), :],
                                 _CONTRACT_LAST, preferred_element_type=F32)
            for j in range(i, nb):
                s = sc[blocks(j - i, j - i + 1), :]
                if j - i < 2:
                    s = s + bias_ref[p, j - i]
                t = jnp.maximum(s[:, :half], s[:, half:])
                m_run[j] = t if m_run[j] is None else jnp.maximum(m_run[j], t)
                r0 = tile_row[i] + (j - i) * L
                sbuf[p, r0:r0 + L, :] = s
        m = [jnp.max(t, axis=-1, keepdims=True) for t in m_run]

        l_run = [None] * nb
        acc = [None] * nb
        for i in range(nb):
            es = []
            for j in range(i, nb):
                r0 = tile_row[i] + (j - i) * L
                e = jnp.exp(sbuf[p, r0:r0 + L, :] - m[j])
                t = e[:, :half] + e[:, half:]
                l_run[j] = t if l_run[j] is None else l_run[j] + t
                es.append(e.astype(BF16))
            pv = jnp.dot(jnp.concatenate(es, axis=0), v_ref[blocks(i, i + 1), :],
                         preferred_element_type=F32)
            for j in range(i, nb):
                t = pv[blocks(j - i, j - i + 1), :]
                acc[j] = t if acc[j] is None else acc[j] + t
        outs.append([acc[j] / jnp.sum(l_run[j], axis=-1, keepdims=True) for j in range(nb)])

    for j in range(nb):
        o_ref[blocks(j, j + 1), :] = jnp.where(head0, outs[0][j], outs[1][j]).astype(BF16)


def _moba(mq, mk, mv, kmrep, bias, onehot, rankmat):
    B, S, _ = mq.shape
    nb = S // MOBA_BLOCK
    slab = pl.BlockSpec((None, S, LANES), lambda b, j: (b, 0, j))
    return pl.pallas_call(
        _moba_kernel,
        grid=(B, N_SLABS),
        in_specs=[slab, slab, slab,
                  pl.BlockSpec((None, HEADS_PER_SLAB, LANES, 2 * LANES), lambda b, j: (b, j, 0, 0)),
                  pl.BlockSpec((HEADS_PER_SLAB, 2, MOBA_BLOCK, MOBA_BLOCK), lambda b, j: (j, 0, 0, 0)),
                  _const_spec((HEADS_PER_SLAB, S, LANES)),
                  _const_spec((HEADS_PER_SLAB, LANES, LANES))],
        out_specs=slab,
        out_shape=jax.ShapeDtypeStruct((B, S, MOBA_WIDTH), BF16),
        scratch_shapes=[pltpu.VMEM((HEADS_PER_SLAB, S, LANES), BF16),
                        pltpu.VMEM((HEADS_PER_SLAB, S, LANES), BF16),
                        pltpu.VMEM((HEADS_PER_SLAB, nb * (nb + 1) // 2 * MOBA_BLOCK, MOBA_BLOCK), F32)],
        compiler_params=pltpu.CompilerParams(
            dimension_semantics=("arbitrary", "arbitrary"), vmem_limit_bytes=VMEM_LIMIT),
        name="moba",
    )(mq, mk, mv, kmrep, bias, onehot, rankmat)


def _mix_ffn_kernel(yr_ref, ym_ref, yc_ref, x_ref, wo_ref, g1_ref, b1_ref,
                    wu_ref, cw_ref, wd_ref, g2_ref, b2_ref, o_ref,
                    abuf, x1_buf, x1b_buf, acc):
    rows = x_ref.shape[0]
    cat = jnp.concatenate([yr_ref[...], ym_ref[...], yc_ref[...]], axis=1)
    mix = jnp.dot(cat, wo_ref[...], preferred_element_type=F32)
    x1 = _layer_norm(DN_ALPHA * x_ref[...] + mix, g1_ref[...], b1_ref[...])
    x1_buf[...] = x1
    x1b_buf[...] = x1.astype(BF16)

    @pl.when(pl.program_id(1) == 0)
    def _():
        abuf[0:SUBLANES, :] = jnp.zeros((SUBLANES, D_FF), F32)

    for c0 in range(0, D_FF, FF_CHUNK):
        c1 = min(c0 + FF_CHUNK, D_FF)
        cols = slice(c0, c1)
        x1b = x1b_buf[...]
        a = jnp.dot(x1b, wu_ref[:, c0:c1], preferred_element_type=F32)
        gate = jnp.dot(x1b, wu_ref[:, D_FF + c0:D_FF + c1], preferred_element_type=F32)
        a = _causal_conv3(abuf, cols, a, cw_ref, rows)
        h = 0.5 * a * (1.0 + lax.erf(a * math.sqrt(0.5))) * gate
        contrib = jnp.dot(h.astype(BF16), wd_ref[cols, :], preferred_element_type=F32)
        if c0 == 0:
            acc[...] = contrib
        else:
            acc[...] += contrib

    o_ref[...] = _layer_norm(DN_ALPHA * x1_buf[...] + acc[...], g2_ref[...], b2_ref[...])


def _mix_ffn(y_ret, y_moba, y_conv, x, w_out, g1, b1, w_up, ffn_cw, w_down, g2, b2):
    B, S, _ = x.shape
    row = lambda w: pl.BlockSpec((None, ROW_TILE, w), lambda b, t: (b, t, 0))
    return pl.pallas_call(
        _mix_ffn_kernel,
        grid=(B, S // ROW_TILE),
        in_specs=[
            row(RET_WIDTH), row(MOBA_WIDTH), row(CONV_WIDTH), row(D_MODEL),
            _const_spec((D_MODEL, D_MODEL)), _const_spec((1, D_MODEL)), _const_spec((1, D_MODEL)),
            _const_spec((D_MODEL, 2 * D_FF)), _const_spec((CONV_K, D_FF)),
            _const_spec((D_FF, D_MODEL)), _const_spec((1, D_MODEL)), _const_spec((1, D_MODEL)),
        ],
        out_specs=row(D_MODEL),
        out_shape=jax.ShapeDtypeStruct((B, S, D_MODEL), F32),
        scratch_shapes=[
            pltpu.VMEM((ROW_TILE + SUBLANES, D_FF), F32),
            pltpu.VMEM((ROW_TILE, D_MODEL), F32),
            pltpu.VMEM((ROW_TILE, D_MODEL), BF16),
            pltpu.VMEM((ROW_TILE, D_MODEL), F32),
        ],
        compiler_params=pltpu.CompilerParams(
            dimension_semantics=("arbitrary", "arbitrary"), vmem_limit_bytes=VMEM_LIMIT),
        name="mix_ffn",
    )(y_ret, y_moba, y_conv, x, w_out, g1, b1, w_up, ffn_cw, w_down, g2, b2)


def _t5_bucket(dist):
    n = jnp.maximum(dist, 0)
    max_exact = REL_BUCKETS // 2
    nf = jnp.maximum(n, 1).astype(F32)
    large = max_exact + (jnp.log(nf / max_exact) / math.log(REL_MAX_DIST / max_exact)
                         * (REL_BUCKETS - max_exact)).astype(jnp.int32)
    large = jnp.minimum(large, REL_BUCKETS - 1)
    return jnp.where(n < max_exact, n, large)


def _rotary_tables(S):
    inv = ROPE_BASE ** (-jnp.arange(0, HEAD_DIM, 2, dtype=F32) / HEAD_DIM)
    ang = jnp.arange(S, dtype=F32)[:, None] * inv[None, :]
    cos, sin = jnp.cos(ang), jnp.sin(ang)
    cos_t = jnp.tile(cos, (1, LANES // (HEAD_DIM // 2)))
    sin_t = jnp.tile(jnp.concatenate([-sin, sin], axis=-1), (1, HEADS_PER_SLAB))
    return cos_t, sin_t


def _retention_tables():
    C = RET_CHUNK
    log_g = jnp.log(1.0 - 2.0 ** (-5.0 - jnp.arange(RET_HEADS, dtype=F32)))
    idx = jnp.arange(C, dtype=F32)
    diff = idx[:, None] - idx[None, :]
    decay = jnp.where(diff >= 0, jnp.exp(jnp.maximum(diff, 0.0)[None] * log_g[:, None, None]), 0.0)
    xi = jnp.exp((idx + 1.0)[None, :] * log_g[:, None])
    zeta = jnp.exp((C - 1.0 - idx)[None, :] * log_g[:, None])
    g_chunk = jnp.exp(C * log_g)
    per_lane = lambda t: jnp.repeat(t.T, HEAD_DIM, axis=1)
    xi_t = per_lane(xi).reshape(C, N_SLABS, LANES).transpose(1, 0, 2)
    zeta_t = jnp.tile(per_lane(zeta), (ROW_TILE // C, 1))
    gc_t = jnp.broadcast_to(
        jnp.repeat(g_chunk, HEAD_DIM).reshape(N_SLABS, 1, LANES), (N_SLABS, LANES, LANES))
    return decay, xi_t, zeta_t, gc_t


def _bias_tables(rel_bias):
    L = MOBA_BLOCK
    rel_t = rel_bias.T.astype(F32)
    pos = jnp.arange(L, dtype=jnp.int32)
    d_own = pos[:, None] - pos[None, :]

    def lookup(dist):
        bucket = _t5_bucket(dist)
        out = jnp.zeros((MOBA_HEADS,) + dist.shape, F32)
        for b in range(REL_BUCKETS):
            out = jnp.where(bucket[None] == b, rel_t[:, b].reshape(-1, 1, 1), out)
        return out

    far = lookup(jnp.full((1, 1), 2 * L, jnp.int32))
    own = jnp.where(d_own[None] >= 0, lookup(d_own) - far, -jnp.inf)
    prev = lookup(d_own + L) - far
    return jnp.stack([own, prev], axis=1)


def _moba_tables(S):
    nb = S // MOBA_BLOCK
    assert nb <= MAX_BLOCKS
    onehot = np.zeros((HEADS_PER_SLAB, S, LANES), np.float32)
    rankmat = np.zeros((HEADS_PER_SLAB, LANES, LANES), np.float32)
    for p in range(HEADS_PER_SLAB):
        for r in range(S):
            onehot[p, r, _aug_lane(p, r // MOBA_BLOCK)] = 1.0
        for m in range(MAX_BLOCKS):
            for n in range(MAX_BLOCKS):
                rankmat[p, m * MAX_BLOCKS + n, _aug_lane(p, n)] = 1.0
    return jnp.asarray(onehot, BF16), jnp.asarray(rankmat, BF16)


def _gate_operand(km):
    B, nb, _ = km.shape
    kmt = km.reshape(B, nb, MOBA_HEADS, HEAD_DIM).transpose(0, 2, 3, 1)
    kmt = jnp.pad(kmt, ((0, 0), (0, 0), (0, 0), (0, MAX_BLOCKS - nb)))
    by_m = jnp.repeat(kmt, MAX_BLOCKS, axis=-1)
    by_n = jnp.tile(kmt, (1, 1, 1, MAX_BLOCKS))
    padc = lambda t: jnp.pad(t, ((0, 0), (0, 0), (0, 0), (0, LANES - t.shape[-1])))
    data = jnp.concatenate([padc(by_m), padc(by_n)], axis=-1)
    zeros = jnp.zeros_like(data)
    first = jnp.concatenate([data, zeros], axis=2)
    second = jnp.concatenate([zeros, data], axis=2)
    parity = (jnp.arange(MOBA_HEADS) % HEADS_PER_SLAB).reshape(1, -1, 1, 1)
    return jnp.where(parity == 0, first, second).astype(BF16)


def kernel(x, w_in, conv_w, w_out, ln1_g, ln1_b, w_up, ffn_conv_w, w_down, ln2_g, ln2_b, rel_bias):
    B, S, _ = x.shape
    cos_t, sin_t = _rotary_tables(S)
    decay, xi_t, zeta_t, gc_t = _retention_tables()
    bias = _bias_tables(rel_bias)
    onehot, rankmat = _moba_tables(S)
    for l in range(DEPTH):
        (qr, kr, kz, vr, gr, mq, mk, mv, km, yc) = _inproj(
            x, w_in[l].astype(BF16), cos_t, sin_t, zeta_t, conv_w[l])
        y_ret = _retention(qr, kr, kz, vr, gr, decay, xi_t, gc_t)
        kmrep = _gate_operand(km.reshape(B, S // MOBA_BLOCK, MOBA_WIDTH))
        y_moba = _moba(mq, mk, mv, kmrep, bias, onehot, rankmat)
        x = _mix_ffn(y_ret, y_moba, yc, x, w_out[l].astype(BF16),
                     ln1_g[l][None], ln1_b[l][None], w_up[l].astype(BF16), ffn_conv_w[l],
                     w_down[l].astype(BF16), ln2_g[l][None], ln2_b[l][None])
    return x
```

```python
import math

import numpy as np
import jax
import jax.numpy as jnp
from jax import lax
from jax.experimental import pallas as pl
from jax.experimental.pallas import tpu as pltpu

F32 = jnp.float32
BF16 = jnp.bfloat16

D_MODEL = 1024
HEAD_DIM = 64
RET_HEADS = 6
MOBA_HEADS = 6
RET_WIDTH = RET_HEADS * HEAD_DIM
MOBA_WIDTH = MOBA_HEADS * HEAD_DIM
CONV_WIDTH = 256
IN_COLS = 4 * RET_WIDTH + 3 * MOBA_WIDTH + 3 * CONV_WIDTH
CONV_K = 3
MOBA_BLOCK = 256
MOBA_TOPK = 3
REL_BUCKETS = 32
REL_MAX_DIST = 128
ROPE_BASE = 10000.0
D_FF = 2816
DEPTH = 2
DN_ALPHA = (2.0 * DEPTH) ** 0.25
LN_EPS = 1e-5
QK_SCALE = HEAD_DIM ** -0.5

LANES = 128
SUBLANES = 8
HEADS_PER_SLAB = LANES // HEAD_DIM
N_SLABS = RET_WIDTH // LANES
ROW_TILE = 512
FFN_ROW_TILE = 512
RET_CHUNK = 256
FF_CHUNK = 512
VMEM_LIMIT = 56 * 1024 * 1024
MASK_PENALTY = -1e30
MAX_BLOCKS = 8

_O = [0]
for _w in (RET_WIDTH,) * 4 + (MOBA_WIDTH,) * 3 + (CONV_WIDTH,) * 3:
    _O.append(_O[-1] + _w)
O_RQ, O_RK, O_RV, O_RG, O_MQ, O_MK, O_MV, O_CB, O_CC, O_CH, _ = _O

_CONTRACT_LAST = (((1,), (1,)), ((), ()))
_CONTRACT_FIRST = (((0,), (0,)), ((), ()))


def _const_spec(shape):
    zeros = (0,) * len(shape)
    return pl.BlockSpec(shape, lambda *_: zeros, pipeline_mode=pl.Buffered(1))


def _layer_norm(v, g, b):
    mu = jnp.mean(v, axis=-1, keepdims=True)
    d = v - mu
    var = jnp.mean(d * d, axis=-1, keepdims=True)
    return d * lax.rsqrt(var + LN_EPS) * g + b


def _causal_conv3(tail, cols, cur, w_ref, rows):
    prev = tail[:, cols]
    sub = lax.broadcasted_iota(jnp.int32, prev.shape, 0)

    def shifted(k):
        body = pltpu.roll(cur, k, 0)
        head = jnp.where(sub < k, pltpu.roll(prev, k, 0), body[0:SUBLANES])
        return jnp.concatenate([head, body[SUBLANES:]], axis=0)

    y = shifted(2) * w_ref[0:1, cols]
    y = y + shifted(1) * w_ref[1:2, cols]
    y = y + cur * w_ref[2:3, cols]
    tail[:, cols] = cur[rows - SUBLANES:rows]
    return y


def _first_head_lanes(rows):
    return lax.broadcasted_iota(jnp.int32, (rows, LANES), 1) < HEAD_DIM


def _inproj_kernel(x_ref, w_ref, cos_ref, sin_ref, zeta_ref, cw_ref,
                   qr_o, kr_o, kz_o, vr_o, gr_o, mq_o, mk_o, mv_o, km_o, yc_o, pbuf):
    rows = x_ref.shape[0]
    xb = x_ref[...].astype(BF16)

    def proj(c0, width):
        return jnp.dot(xb, w_ref[:, c0:c0 + width], preferred_element_type=F32)

    cos = cos_ref[...]
    sin = sin_ref[...]
    lane = lax.broadcasted_iota(jnp.int32, (rows, LANES), 1)
    first_half = (lane & (HEAD_DIM // 2)) == 0

    def rotary_slab(u):
        partner = jnp.where(first_half,
                            pltpu.roll(u, LANES - HEAD_DIM // 2, 1),
                            pltpu.roll(u, HEAD_DIM // 2, 1))
        return u * cos + partner * sin

    qk = proj(O_RQ, 2 * RET_WIDTH)
    for s in range(N_SLABS):
        cs = slice(s * LANES, (s + 1) * LANES)
        q = rotary_slab(qk[:, cs])
        qr_o[:, cs] = q.astype(BF16)
        k = rotary_slab(qk[:, RET_WIDTH + s * LANES:RET_WIDTH + (s + 1) * LANES]) * QK_SCALE
        kr_o[:, cs] = k.astype(BF16)
        kz_o[:, cs] = (k * zeta_ref[:, cs]).astype(BF16)

    vg = proj(O_RV, 2 * RET_WIDTH)
    vr_o[...] = vg[:, :RET_WIDTH].astype(BF16)
    rg = vg[:, RET_WIDTH:]
    gr_o[...] = rg * jax.nn.sigmoid(rg)
    mqk = proj(O_MQ, 2 * MOBA_WIDTH)
    mq_o[...] = (mqk[:, :MOBA_WIDTH] * QK_SCALE).astype(BF16)
    mk = mqk[:, MOBA_WIDTH:]
    mk_o[...] = mk.astype(BF16)
    km_o[...] = jnp.mean(mk.reshape(rows // MOBA_BLOCK, MOBA_BLOCK, MOBA_WIDTH), axis=1)
    rest = proj(O_MV, MOBA_WIDTH + 3 * CONV_WIDTH)
    mv_o[...] = rest[:, :MOBA_WIDTH].astype(BF16)

    @pl.when(pl.program_id(1) == 0)
    def _():
        pbuf[...] = jnp.zeros((SUBLANES, CONV_WIDTH), F32)

    cb = rest[:, MOBA_WIDTH:MOBA_WIDTH + CONV_WIDTH]
    p = (rest[:, MOBA_WIDTH + CONV_WIDTH:MOBA_WIDTH + 2 * CONV_WIDTH]
         * rest[:, MOBA_WIDTH + 2 * CONV_WIDTH:])
    y = _causal_conv3(pbuf, slice(0, CONV_WIDTH), p, cw_ref, rows)
    yc_o[...] = (cb * y).astype(BF16)


def _inproj(x, w_in, cos, sin, zeta, conv_w):
    B, S, _ = x.shape
    nt = S // ROW_TILE
    nblk = ROW_TILE // MOBA_BLOCK
    row = lambda w: pl.BlockSpec((None, ROW_TILE, w), lambda b, t: (b, t, 0))
    sds = jax.ShapeDtypeStruct
    return pl.pallas_call(
        _inproj_kernel,
        grid=(B, nt),
        in_specs=[
            row(D_MODEL),
            _const_spec((D_MODEL, IN_COLS)),
            pl.BlockSpec((ROW_TILE, LANES), lambda b, t: (t, 0)),
            pl.BlockSpec((ROW_TILE, LANES), lambda b, t: (t, 0)),
            _const_spec((ROW_TILE, RET_WIDTH)),
            _const_spec((CONV_K, CONV_WIDTH)),
        ],
        out_specs=[
            row(RET_WIDTH), row(RET_WIDTH), row(RET_WIDTH), row(RET_WIDTH), row(RET_WIDTH),
            row(MOBA_WIDTH), row(MOBA_WIDTH), row(MOBA_WIDTH),
            pl.BlockSpec((None, None, nblk, MOBA_WIDTH), lambda b, t: (b, t, 0, 0)),
            row(CONV_WIDTH),
        ],
        out_shape=[
            sds((B, S, RET_WIDTH), BF16), sds((B, S, RET_WIDTH), BF16),
            sds((B, S, RET_WIDTH), BF16), sds((B, S, RET_WIDTH), BF16),
            sds((B, S, RET_WIDTH), F32),
            sds((B, S, MOBA_WIDTH), BF16), sds((B, S, MOBA_WIDTH), BF16),
            sds((B, S, MOBA_WIDTH), BF16),
            sds((B, nt, nblk, MOBA_WIDTH), F32),
            sds((B, S, CONV_WIDTH), BF16),
        ],
        scratch_shapes=[pltpu.VMEM((SUBLANES, CONV_WIDTH), F32)],
        compiler_params=pltpu.CompilerParams(
            dimension_semantics=("arbitrary", "arbitrary"), vmem_limit_bytes=VMEM_LIMIT),
        name="inproj",
    )(x, w_in, cos, sin, zeta, conv_w)


def _retention_kernel(q_ref, k_ref, kz_ref, v_ref, g_ref, dec_ref, xi_ref, gc_ref, o_ref):
    C = RET_CHUNK
    n_chunks = q_ref.shape[0] // C
    head0 = _first_head_lanes(C)
    r = lax.broadcasted_iota(jnp.int32, (LANES, LANES), 0) < HEAD_DIM
    c = lax.broadcasted_iota(jnp.int32, (LANES, LANES), 1) < HEAD_DIM
    same_head = r == c
    inv_dim = 1.0 / HEAD_DIM

    def head_mean(t):
        s0 = jnp.sum(jnp.where(head0, t, 0.0), axis=-1, keepdims=True)
        s1 = jnp.sum(jnp.where(head0, 0.0, t), axis=-1, keepdims=True)
        return jnp.where(head0, s0, s1) * inv_dim

    state = jnp.zeros((LANES, LANES), F32)
    for ci in range(n_chunks):
        rs = slice(ci * C, (ci + 1) * C)
        q = q_ref[rs, :]
        k = k_ref[rs, :]
        v = v_ref[rs, :]
        inner = []
        for hh in range(HEADS_PER_SLAB):
            kh = jnp.where(head0 if hh == 0 else jnp.logical_not(head0), k, jnp.zeros_like(k))
            scores = lax.dot_general(q, kh, _CONTRACT_LAST, preferred_element_type=F32) * dec_ref[hh]
            inner.append(jnp.dot(scores.astype(BF16), v, preferred_element_type=F32))
        inner = jnp.where(head0, inner[0], inner[1])
        cross = jnp.dot(q, state.astype(BF16), preferred_element_type=F32) * xi_ref[...]
        o = inner + cross
        kv = lax.dot_general(kz_ref[rs, :], v, _CONTRACT_FIRST, preferred_element_type=F32)
        state = state * gc_ref[...] + jnp.where(same_head, kv, 0.0)
        d = o - head_mean(o)
        var = head_mean(d * d)
        y = d * lax.rsqrt(var + LN_EPS) * g_ref[rs, :]
        o_ref[rs, :] = y.astype(BF16)


def _retention(qr, kr, kz, vr, gr, decay, xi, gc):
    B, S, _ = qr.shape
    slab = pl.BlockSpec((None, S, LANES), lambda b, j: (b, 0, j))
    return pl.pallas_call(
        _retention_kernel,
        grid=(B, N_SLABS),
        in_specs=[slab, slab, slab, slab, slab,
                  pl.BlockSpec((HEADS_PER_SLAB, RET_CHUNK, RET_CHUNK), lambda b, j: (j, 0, 0)),
                  pl.BlockSpec((None, RET_CHUNK, LANES), lambda b, j: (j, 0, 0)),
                  pl.BlockSpec((None, LANES, LANES), lambda b, j: (j, 0, 0))],
        out_specs=slab,
        out_shape=jax.ShapeDtypeStruct((B, S, RET_WIDTH), BF16),
        compiler_params=pltpu.CompilerParams(
            dimension_semantics=("arbitrary", "arbitrary"), vmem_limit_bytes=VMEM_LIMIT),
        name="retention",
    )(qr, kr, kz, vr, gr, decay, xi, gc)


def _aug_lane(p, n):
    return (HEAD_DIM if p == 0 else 0) + n


def _moba_kernel(q_ref, k_ref, v_ref, km_ref, bias_ref, oh_ref, rk_ref, o_ref, qaug, kaug, sbuf):
    L = MOBA_BLOCK
    S = q_ref.shape[0]
    nb = S // L
    half = L // 2
    first_ranked = min(MOBA_TOPK + 1, nb)
    head0 = _first_head_lanes(L)
    lane = lax.broadcasted_iota(jnp.int32, (L, LANES), 1)
    pair_m = lane >> 3
    pair_n = lane & 7
    m_before_n = pair_m < pair_n
    blocks = lambda i0, i1: slice(i0 * L, i1 * L)
    tile_row = [0]
    for i in range(nb):
        tile_row.append(tile_row[-1] + (nb - i) * L)

    outs = []
    for p in range(HEADS_PER_SLAB):
        is_data = head0 if p == 0 else jnp.logical_not(head0)

        for i in range(nb):
            kaug[p, blocks(i, i + 1), :] = jnp.where(is_data, k_ref[blocks(i, i + 1), :],
                                                     oh_ref[p, blocks(i, i + 1), :])
        for i in range(first_ranked):
            qaug[p, blocks(i, i + 1), :] = jnp.where(is_data, q_ref[blocks(i, i + 1), :],
                                                     jnp.zeros((L, LANES), BF16))
        if first_ranked < nb:
            g = jnp.dot(q_ref[blocks(first_ranked, nb), :], km_ref[p], preferred_element_type=F32)
            beats = []
            for i in range(first_ranked, nb):
                gi = g[blocks(i - first_ranked, i - first_ranked + 1), :]
                gm = gi[:, :LANES]
                gn = gi[:, LANES:]
                valid = (lane < MAX_BLOCKS * MAX_BLOCKS) & (pair_m < i) & (pair_n < i)
                b = ((gm > gn) | ((gm == gn) & m_before_n)) & valid
                beats.append(jnp.where(b, 1.0, 0.0).astype(BF16))
            rank = jnp.dot(jnp.concatenate(beats, axis=0), rk_ref[p], preferred_element_type=F32)
            for i in range(first_ranked, nb):
                r = rank[blocks(i - first_ranked, i - first_ranked + 1), :]
                pen = jnp.where(r >= MOBA_TOPK, MASK_PENALTY, 0.0).astype(BF16)
                qaug[p, blocks(i, i + 1), :] = jnp.where(is_data, q_ref[blocks(i, i + 1), :], pen)

        for i in range(nb):
            sc = lax.dot_general(qaug[p, blocks(i, nb), :], kaug[p, blocks(i, i + 1), :],
                                 _CONTRACT_LAST, preferred_element_type=F32)
            for j in range(i, nb):
                s = sc[blocks(j - i, j - i + 1), :]
                if j - i < 2:
                    s = s + bias_ref[p, j - i]
                r0 = tile_row[i] + (j - i) * L
                sbuf[p, r0:r0 + L, :] = s

        head_out = []
        for j in range(nb):
            tiles = [tile_row[i] + (j - i) * L for i in range(j + 1)]
            m_run = None
            for r0 in tiles:
                s = sbuf[p, r0:r0 + L, :]
                t = jnp.maximum(s[:, :half], s[:, half:])
                m_run = t if m_run is None else jnp.maximum(m_run, t)
            m = jnp.max(m_run, axis=-1, keepdims=True)
            l_run = None
            acc = None
            for i, r0 in enumerate(tiles):
                e = jnp.exp(sbuf[p, r0:r0 + L, :] - m)
                t = e[:, :half] + e[:, half:]
                l_run = t if l_run is None else l_run + t
                pv = jnp.dot(e.astype(BF16), v_ref[blocks(i, i + 1), :], preferred_element_type=F32)
                acc = pv if acc is None else acc + pv
            head_out.append(acc / jnp.sum(l_run, axis=-1, keepdims=True))
        outs.append(head_out)

    for j in range(nb):
        o_ref[blocks(j, j + 1), :] = jnp.where(head0, outs[0][j], outs[1][j]).astype(BF16)


def _moba(mq, mk, mv, kmrep, bias, onehot, rankmat):
    B, S, _ = mq.shape
    nb = S // MOBA_BLOCK
    slab = pl.BlockSpec((None, S, LANES), lambda b, j: (b, 0, j))
    return pl.pallas_call(
        _moba_kernel,
        grid=(B, N_SLABS),
        in_specs=[slab, slab, slab,
                  pl.BlockSpec((None, HEADS_PER_SLAB, LANES, 2 * LANES), lambda b, j: (b, j, 0, 0)),
                  pl.BlockSpec((HEADS_PER_SLAB, 2, MOBA_BLOCK, MOBA_BLOCK), lambda b, j: (j, 0, 0, 0)),
                  _const_spec((HEADS_PER_SLAB, S, LANES)),
                  _const_spec((HEADS_PER_SLAB, LANES, LANES))],
        out_specs=slab,
        out_shape=jax.ShapeDtypeStruct((B, S, MOBA_WIDTH), BF16),
        scratch_shapes=[pltpu.VMEM((HEADS_PER_SLAB, S, LANES), BF16),
                        pltpu.VMEM((HEADS_PER_SLAB, S, LANES), BF16),
                        pltpu.VMEM((HEADS_PER_SLAB, nb * (nb + 1) // 2 * MOBA_BLOCK, MOBA_BLOCK), F32)],
        compiler_params=pltpu.CompilerParams(
            dimension_semantics=("arbitrary", "arbitrary"), vmem_limit_bytes=VMEM_LIMIT),
        name="moba",
    )(mq, mk, mv, kmrep, bias, onehot, rankmat)


def _mix_ffn_kernel(yr_ref, ym_ref, yc_ref, x_ref, wo_ref, g1_ref, b1_ref,
                    wu_ref, cw_ref, wd_ref, g2_ref, b2_ref, o_ref,
                    abuf, x1_buf, x1b_buf, acc):
    rows = x_ref.shape[0]
    cat = jnp.concatenate([yr_ref[...], ym_ref[...], yc_ref[...]], axis=1)
    mix = jnp.dot(cat, wo_ref[...], preferred_element_type=F32)
    x1 = _layer_norm(DN_ALPHA * x_ref[...] + mix, g1_ref[...], b1_ref[...])
    x1_buf[...] = x1
    x1b_buf[...] = x1.astype(BF16)

    @pl.when(pl.program_id(1) == 0)
    def _():
        abuf[...] = jnp.zeros((SUBLANES, D_FF), F32)

    def up(c0, c1):
        x1b = x1b_buf[...]
        return (jnp.dot(x1b, wu_ref[:, c0:c1], preferred_element_type=F32),
                jnp.dot(x1b, wu_ref[:, D_FF + c0:D_FF + c1], preferred_element_type=F32))

    chunks = [(c0, min(c0 + FF_CHUNK, D_FF)) for c0 in range(0, D_FF, FF_CHUNK)]
    ahead = up(*chunks[0])
    for n, (c0, c1) in enumerate(chunks):
        cols = slice(c0, c1)
        a, gate = ahead
        if n + 1 < len(chunks):
            ahead = up(*chunks[n + 1])
        a = _causal_conv3(abuf, cols, a, cw_ref, rows)
        h = 0.5 * a * (1.0 + lax.erf(a * math.sqrt(0.5))) * gate
        contrib = jnp.dot(h.astype(BF16), wd_ref[cols, :], preferred_element_type=F32)
        if c0 == 0:
            acc[...] = contrib
        else:
            acc[...] += contrib

    o_ref[...] = _layer_norm(DN_ALPHA * x1_buf[...] + acc[...], g2_ref[...], b2_ref[...])


def _mix_ffn(y_ret, y_moba, y_conv, x, w_out, g1, b1, w_up, ffn_cw, w_down, g2, b2):
    B, S, _ = x.shape
    row = lambda w: pl.BlockSpec((None, FFN_ROW_TILE, w), lambda b, t: (b, t, 0))
    return pl.pallas_call(
        _mix_ffn_kernel,
        grid=(B, S // FFN_ROW_TILE),
        in_specs=[
            row(RET_WIDTH), row(MOBA_WIDTH), row(CONV_WIDTH), row(D_MODEL),
            _const_spec((D_MODEL, D_MODEL)), _const_spec((1, D_MODEL)), _const_spec((1, D_MODEL)),
            _const_spec((D_MODEL, 2 * D_FF)), _const_spec((CONV_K, D_FF)),
            _const_spec((D_FF, D_MODEL)), _const_spec((1, D_MODEL)), _const_spec((1, D_MODEL)),
        ],
        out_specs=row(D_MODEL),
        out_shape=jax.ShapeDtypeStruct((B, S, D_MODEL), F32),
        scratch_shapes=[
            pltpu.VMEM((SUBLANES, D_FF), F32),
            pltpu.VMEM((FFN_ROW_TILE, D_MODEL), F32),
            pltpu.VMEM((FFN_ROW_TILE, D_MODEL), BF16),
            pltpu.VMEM((FFN_ROW_TILE, D_MODEL), F32),
        ],
        compiler_params=pltpu.CompilerParams(
            dimension_semantics=("arbitrary", "arbitrary"), vmem_limit_bytes=VMEM_LIMIT),
        name="mix_ffn",
    )(y_ret, y_moba, y_conv, x, w_out, g1, b1, w_up, ffn_cw, w_down, g2, b2)


def _t5_bucket(dist):
    n = jnp.maximum(dist, 0)
    max_exact = REL_BUCKETS // 2
    nf = jnp.maximum(n, 1).astype(F32)
    large = max_exact + (jnp.log(nf / max_exact) / math.log(REL_MAX_DIST / max_exact)
                         * (REL_BUCKETS - max_exact)).astype(jnp.int32)
    large = jnp.minimum(large, REL_BUCKETS - 1)
    return jnp.where(n < max_exact, n, large)


def _rotary_tables(S):
    inv = ROPE_BASE ** (-jnp.arange(0, HEAD_DIM, 2, dtype=F32) / HEAD_DIM)
    ang = jnp.arange(S, dtype=F32)[:, None] * inv[None, :]
    cos, sin = jnp.cos(ang), jnp.sin(ang)
    cos_t = jnp.tile(cos, (1, LANES // (HEAD_DIM // 2)))
    sin_t = jnp.tile(jnp.concatenate([-sin, sin], axis=-1), (1, HEADS_PER_SLAB))
    return cos_t, sin_t


def _retention_tables():
    C = RET_CHUNK
    log_g = jnp.log(1.0 - 2.0 ** (-5.0 - jnp.arange(RET_HEADS, dtype=F32)))
    idx = jnp.arange(C, dtype=F32)
    diff = idx[:, None] - idx[None, :]
    decay = jnp.where(diff >= 0, jnp.exp(jnp.maximum(diff, 0.0)[None] * log_g[:, None, None]), 0.0)
    xi = jnp.exp((idx + 1.0)[None, :] * log_g[:, None])
    zeta = jnp.exp((C - 1.0 - idx)[None, :] * log_g[:, None])
    g_chunk = jnp.exp(C * log_g)
    per_lane = lambda t: jnp.repeat(t.T, HEAD_DIM, axis=1)
    xi_t = per_lane(xi).reshape(C, N_SLABS, LANES).transpose(1, 0, 2)
    zeta_t = jnp.tile(per_lane(zeta), (ROW_TILE // C, 1))
    gc_t = jnp.broadcast_to(
        jnp.repeat(g_chunk, HEAD_DIM).reshape(N_SLABS, 1, LANES), (N_SLABS, LANES, LANES))
    return decay, xi_t, zeta_t, gc_t


def _bias_tables(rel_bias):
    L = MOBA_BLOCK
    rel_t = rel_bias.T.astype(F32)
    pos = jnp.arange(L, dtype=jnp.int32)
    d_own = pos[:, None] - pos[None, :]

    def lookup(dist):
        bucket = _t5_bucket(dist)
        out = jnp.zeros((MOBA_HEADS,) + dist.shape, F32)
        for b in range(REL_BUCKETS):
            out = jnp.where(bucket[None] == b, rel_t[:, b].reshape(-1, 1, 1), out)
        return out

    far = lookup(jnp.full((1, 1), 2 * L, jnp.int32))
    own = jnp.where(d_own[None] >= 0, lookup(d_own) - far, -jnp.inf)
    prev = lookup(d_own + L) - far
    return jnp.stack([own, prev], axis=1)


def _moba_tables(S):
    nb = S // MOBA_BLOCK
    assert nb <= MAX_BLOCKS
    onehot = np.zeros((HEADS_PER_SLAB, S, LANES), np.float32)
    rankmat = np.zeros((HEADS_PER_SLAB, LANES, LANES), np.float32)
    for p in range(HEADS_PER_SLAB):
        for r in range(S):
            onehot[p, r, _aug_lane(p, r // MOBA_BLOCK)] = 1.0
        for m in range(MAX_BLOCKS):
            for n in range(MAX_BLOCKS):
                rankmat[p, m * MAX_BLOCKS + n, _aug_lane(p, n)] = 1.0
    return jnp.asarray(onehot, BF16), jnp.asarray(rankmat, BF16)


def _gate_operand(km):
    B, nb, _ = km.shape
    kmt = km.reshape(B, nb, MOBA_HEADS, HEAD_DIM).transpose(0, 2, 3, 1)
    kmt = jnp.pad(kmt, ((0, 0), (0, 0), (0, 0), (0, MAX_BLOCKS - nb)))
    by_m = jnp.repeat(kmt, MAX_BLOCKS, axis=-1)
    by_n = jnp.tile(kmt, (1, 1, 1, MAX_BLOCKS))
    padc = lambda t: jnp.pad(t, ((0, 0), (0, 0), (0, 0), (0, LANES - t.shape[-1])))
    data = jnp.concatenate([padc(by_m), padc(by_n)], axis=-1)
    zeros = jnp.zeros_like(data)
    first = jnp.concatenate([data, zeros], axis=2)
    second = jnp.concatenate([zeros, data], axis=2)
    parity = (jnp.arange(MOBA_HEADS) % HEADS_PER_SLAB).reshape(1, -1, 1, 1)
    return jnp.where(parity == 0, first, second).astype(BF16)


def kernel(x, w_in, conv_w, w_out, ln1_g, ln1_b, w_up, ffn_conv_w, w_down, ln2_g, ln2_b, rel_bias):
    B, S, _ = x.shape
    cos_t, sin_t = _rotary_tables(S)
    decay, xi_t, zeta_t, gc_t = _retention_tables()
    bias = _bias_tables(rel_bias)
    onehot, rankmat = _moba_tables(S)
    for l in range(DEPTH):
        (qr, kr, kz, vr, gr, mq, mk, mv, km, yc) = _inproj(
            x, w_in[l].astype(BF16), cos_t, sin_t, zeta_t, conv_w[l])
        y_ret = _retention(qr, kr, kz, vr, gr, decay, xi_t, gc_t)
        kmrep = _gate_operand(km.reshape(B, S // MOBA_BLOCK, MOBA_WIDTH))
        y_moba = _moba(mq, mk, mv, kmrep, bias, onehot, rankmat)
        x = _mix_ffn(y_ret, y_moba, yc, x, w_out[l].astype(BF16),
                     ln1_g[l][None], ln1_b[l][None], w_up[l].astype(BF16), ffn_conv_w[l],
                     w_down[l].astype(BF16), ln2_g[l][None], ln2_b[l][None])
    return x
```

```python
import math

import numpy as np
import jax
import jax.numpy as jnp
from jax import lax
from jax.experimental import pallas as pl
from jax.experimental.pallas import tpu as pltpu

F32 = jnp.float32
BF16 = jnp.bfloat16

D_MODEL = 1024
HEAD_DIM = 64
RET_HEADS = 6
MOBA_HEADS = 6
RET_WIDTH = RET_HEADS * HEAD_DIM
MOBA_WIDTH = MOBA_HEADS * HEAD_DIM
CONV_WIDTH = 256
IN_COLS = 4 * RET_WIDTH + 3 * MOBA_WIDTH + 3 * CONV_WIDTH
CONV_K = 3
MOBA_BLOCK = 256
MOBA_TOPK = 3
REL_BUCKETS = 32
REL_MAX_DIST = 128
ROPE_BASE = 10000.0
D_FF = 2816
DEPTH = 2
DN_ALPHA = (2.0 * DEPTH) ** 0.25
LN_EPS = 1e-5
QK_SCALE = HEAD_DIM ** -0.5

LANES = 128
SUBLANES = 8
HEADS_PER_SLAB = LANES // HEAD_DIM
N_SLABS = RET_WIDTH // LANES
ROW_TILE = 512
FFN_ROW_TILE = 1024
FFN_SUB_ROWS = 512
RET_CHUNK = 256
FF_SUB = 256
VMEM_LIMIT = 60 * 1024 * 1024
MASK_PENALTY = -1e30
MAX_BLOCKS = 8

_O = [0]
for _w in (RET_WIDTH,) * 4 + (MOBA_WIDTH,) * 3 + (CONV_WIDTH,) * 3:
    _O.append(_O[-1] + _w)
O_RQ, O_RK, O_RV, O_RG, O_MQ, O_MK, O_MV, O_CB, O_CC, O_CH, _ = _O

_CONTRACT_LAST = (((1,), (1,)), ((), ()))
_CONTRACT_FIRST = (((0,), (0,)), ((), ()))


def _const_spec(shape):
    zeros = (0,) * len(shape)
    return pl.BlockSpec(shape, lambda *_: zeros, pipeline_mode=pl.Buffered(1))


def _layer_norm(v, g, b):
    mu = jnp.mean(v, axis=-1, keepdims=True)
    d = v - mu
    var = jnp.mean(d * d, axis=-1, keepdims=True)
    return d * lax.rsqrt(var + LN_EPS) * g + b


def _causal_conv3(tail, cols, cur, w_ref, rows):
    prev = tail[:, cols]
    sub = lax.broadcasted_iota(jnp.int32, prev.shape, 0)

    def shifted(k):
        body = pltpu.roll(cur, k, 0)
        head = jnp.where(sub < k, pltpu.roll(prev, k, 0), body[0:SUBLANES])
        return jnp.concatenate([head, body[SUBLANES:]], axis=0)

    y = shifted(2) * w_ref[0:1, cols]
    y = y + shifted(1) * w_ref[1:2, cols]
    y = y + cur * w_ref[2:3, cols]
    tail[:, cols] = cur[rows - SUBLANES:rows]
    return y


def _first_head_lanes(rows):
    return lax.broadcasted_iota(jnp.int32, (rows, LANES), 1) < HEAD_DIM


def _inproj_kernel(x_ref, w_ref, cos_ref, sin_ref, zeta_ref, cw_ref,
                   qr_o, kr_o, kz_o, vr_o, gr_o, mq_o, mk_o, mv_o, km_o, yc_o, pbuf):
    rows = x_ref.shape[0]
    xb = x_ref[...].astype(BF16)

    def proj(c0, width):
        return jnp.dot(xb, w_ref[:, c0:c0 + width], preferred_element_type=F32)

    cos = cos_ref[...]
    sin = sin_ref[...]
    lane = lax.broadcasted_iota(jnp.int32, (rows, LANES), 1)
    first_half = (lane & (HEAD_DIM // 2)) == 0

    def rotary_slab(u):
        partner = jnp.where(first_half,
                            pltpu.roll(u, LANES - HEAD_DIM // 2, 1),
                            pltpu.roll(u, HEAD_DIM // 2, 1))
        return u * cos + partner * sin

    qk = proj(O_RQ, 2 * RET_WIDTH)
    for s in range(N_SLABS):
        cs = slice(s * LANES, (s + 1) * LANES)
        q = rotary_slab(qk[:, cs])
        qr_o[:, cs] = q.astype(BF16)
        k = rotary_slab(qk[:, RET_WIDTH + s * LANES:RET_WIDTH + (s + 1) * LANES]) * QK_SCALE
        kr_o[:, cs] = k.astype(BF16)
        kz_o[:, cs] = (k * zeta_ref[:, cs]).astype(BF16)

    vg = proj(O_RV, 2 * RET_WIDTH)
    vr_o[...] = vg[:, :RET_WIDTH].astype(BF16)
    rg = vg[:, RET_WIDTH:]
    gr_o[...] = rg * jax.nn.sigmoid(rg)
    mqk = proj(O_MQ, 2 * MOBA_WIDTH)
    mq_o[...] = (mqk[:, :MOBA_WIDTH] * QK_SCALE).astype(BF16)
    mk = mqk[:, MOBA_WIDTH:]
    mk_o[...] = mk.astype(BF16)
    km_o[...] = jnp.mean(mk.reshape(rows // MOBA_BLOCK, MOBA_BLOCK, MOBA_WIDTH), axis=1)
    rest = proj(O_MV, MOBA_WIDTH + 3 * CONV_WIDTH)
    mv_o[...] = rest[:, :MOBA_WIDTH].astype(BF16)

    @pl.when(pl.program_id(1) == 0)
    def _():
        pbuf[...] = jnp.zeros((SUBLANES, CONV_WIDTH), F32)

    cb = rest[:, MOBA_WIDTH:MOBA_WIDTH + CONV_WIDTH]
    p = (rest[:, MOBA_WIDTH + CONV_WIDTH:MOBA_WIDTH + 2 * CONV_WIDTH]
         * rest[:, MOBA_WIDTH + 2 * CONV_WIDTH:])
    y = _causal_conv3(pbuf, slice(0, CONV_WIDTH), p, cw_ref, rows)
    yc_o[...] = (cb * y).astype(BF16)


def _inproj(x, w_in, cos, sin, zeta, conv_w):
    B, S, _ = x.shape
    nt = S // ROW_TILE
    nblk = ROW_TILE // MOBA_BLOCK
    row = lambda w: pl.BlockSpec((None, ROW_TILE, w), lambda b, t: (b, t, 0))
    sds = jax.ShapeDtypeStruct
    return pl.pallas_call(
        _inproj_kernel,
        grid=(B, nt),
        in_specs=[
            row(D_MODEL),
            _const_spec((D_MODEL, IN_COLS)),
            pl.BlockSpec((ROW_TILE, LANES), lambda b, t: (t, 0)),
            pl.BlockSpec((ROW_TILE, LANES), lambda b, t: (t, 0)),
            _const_spec((ROW_TILE, RET_WIDTH)),
            _const_spec((CONV_K, CONV_WIDTH)),
        ],
        out_specs=[
            row(RET_WIDTH), row(RET_WIDTH), row(RET_WIDTH), row(RET_WIDTH), row(RET_WIDTH),
            row(MOBA_WIDTH), row(MOBA_WIDTH), row(MOBA_WIDTH),
            pl.BlockSpec((None, None, nblk, MOBA_WIDTH), lambda b, t: (b, t, 0, 0)),
            row(CONV_WIDTH),
        ],
        out_shape=[
            sds((B, S, RET_WIDTH), BF16), sds((B, S, RET_WIDTH), BF16),
            sds((B, S, RET_WIDTH), BF16), sds((B, S, RET_WIDTH), BF16),
            sds((B, S, RET_WIDTH), F32),
            sds((B, S, MOBA_WIDTH), BF16), sds((B, S, MOBA_WIDTH), BF16),
            sds((B, S, MOBA_WIDTH), BF16),
            sds((B, nt, nblk, MOBA_WIDTH), F32),
            sds((B, S, CONV_WIDTH), BF16),
        ],
        scratch_shapes=[pltpu.VMEM((SUBLANES, CONV_WIDTH), F32)],
        compiler_params=pltpu.CompilerParams(
            dimension_semantics=("arbitrary", "arbitrary"), vmem_limit_bytes=VMEM_LIMIT),
        name="inproj",
    )(x, w_in, cos, sin, zeta, conv_w)


def _retention_kernel(q_ref, k_ref, kz_ref, v_ref, g_ref, dec_ref, xi_ref, gc_ref, o_ref):
    C = RET_CHUNK
    n_chunks = q_ref.shape[0] // C
    head0 = _first_head_lanes(C)
    r = lax.broadcasted_iota(jnp.int32, (LANES, LANES), 0) < HEAD_DIM
    c = lax.broadcasted_iota(jnp.int32, (LANES, LANES), 1) < HEAD_DIM
    same_head = r == c
    inv_dim = 1.0 / HEAD_DIM

    def head_mean(t):
        s0 = jnp.sum(jnp.where(head0, t, 0.0), axis=-1, keepdims=True)
        s1 = jnp.sum(jnp.where(head0, 0.0, t), axis=-1, keepdims=True)
        return jnp.where(head0, s0, s1) * inv_dim

    state = jnp.zeros((LANES, LANES), F32)
    for ci in range(n_chunks):
        rs = slice(ci * C, (ci + 1) * C)
        q = q_ref[rs, :]
        k = k_ref[rs, :]
        v = v_ref[rs, :]
        inner = []
        for hh in range(HEADS_PER_SLAB):
            kh = jnp.where(head0 if hh == 0 else jnp.logical_not(head0), k, jnp.zeros_like(k))
            scores = lax.dot_general(q, kh, _CONTRACT_LAST, preferred_element_type=F32) * dec_ref[hh]
            inner.append(jnp.dot(scores.astype(BF16), v, preferred_element_type=F32))
        inner = jnp.where(head0, inner[0], inner[1])
        cross = jnp.dot(q, state.astype(BF16), preferred_element_type=F32) * xi_ref[...]
        o = inner + cross
        kv = lax.dot_general(kz_ref[rs, :], v, _CONTRACT_FIRST, preferred_element_type=F32)
        state = state * gc_ref[...] + jnp.where(same_head, kv, 0.0)
        d = o - head_mean(o)
        var = head_mean(d * d)
        y = d * lax.rsqrt(var + LN_EPS) * g_ref[rs, :]
        o_ref[rs, :] = y.astype(BF16)


def _retention(qr, kr, kz, vr, gr, decay, xi, gc):
    B, S, _ = qr.shape
    slab = pl.BlockSpec((None, S, LANES), lambda b, j: (b, 0, j))
    return pl.pallas_call(
        _retention_kernel,
        grid=(B, N_SLABS),
        in_specs=[slab, slab, slab, slab, slab,
                  pl.BlockSpec((HEADS_PER_SLAB, RET_CHUNK, RET_CHUNK), lambda b, j: (j, 0, 0)),
                  pl.BlockSpec((None, RET_CHUNK, LANES), lambda b, j: (j, 0, 0)),
                  pl.BlockSpec((None, LANES, LANES), lambda b, j: (j, 0, 0))],
        out_specs=slab,
        out_shape=jax.ShapeDtypeStruct((B, S, RET_WIDTH), BF16),
        compiler_params=pltpu.CompilerParams(
            dimension_semantics=("arbitrary", "arbitrary"), vmem_limit_bytes=VMEM_LIMIT),
        name="retention",
    )(qr, kr, kz, vr, gr, decay, xi, gc)


def _aug_lane(p, n):
    return (HEAD_DIM if p == 0 else 0) + n


def _moba_kernel(q_ref, k_ref, v_ref, km_ref, bias_ref, oh_ref, rk_ref, o_ref, qaug, kaug, sbuf):
    L = MOBA_BLOCK
    S = q_ref.shape[0]
    nb = S // L
    half = L // 2
    first_ranked = min(MOBA_TOPK + 1, nb)
    head0 = _first_head_lanes(L)
    lane = lax.broadcasted_iota(jnp.int32, (L, LANES), 1)
    pair_m = lane >> 3
    pair_n = lane & 7
    m_before_n = pair_m < pair_n
    blocks = lambda i0, i1: slice(i0 * L, i1 * L)
    tile_row = [0]
    for i in range(nb):
        tile_row.append(tile_row[-1] + (nb - i) * L)

    outs = []
    for p in range(HEADS_PER_SLAB):
        is_data = head0 if p == 0 else jnp.logical_not(head0)

        for i in range(nb):
            kaug[p, blocks(i, i + 1), :] = jnp.where(is_data, k_ref[blocks(i, i + 1), :],
                                                     oh_ref[p, blocks(i, i + 1), :])
        for i in range(first_ranked):
            qaug[p, blocks(i, i + 1), :] = jnp.where(is_data, q_ref[blocks(i, i + 1), :],
                                                     jnp.zeros((L, LANES), BF16))
        if first_ranked < nb:
            g = jnp.dot(q_ref[blocks(first_ranked, nb), :], km_ref[p], preferred_element_type=F32)
            beats = []
            for i in range(first_ranked, nb):
                gi = g[blocks(i - first_ranked, i - first_ranked + 1), :]
                gm = gi[:, :LANES]
                gn = gi[:, LANES:]
                valid = (lane < MAX_BLOCKS * MAX_BLOCKS) & (pair_m < i) & (pair_n < i)
                b = ((gm > gn) | ((gm == gn) & m_before_n)) & valid
                beats.append(jnp.where(b, 1.0, 0.0).astype(BF16))
            rank = jnp.dot(jnp.concatenate(beats, axis=0), rk_ref[p], preferred_element_type=F32)
            for i in range(first_ranked, nb):
                r = rank[blocks(i - first_ranked, i - first_ranked + 1), :]
                pen = jnp.where(r >= MOBA_TOPK, MASK_PENALTY, 0.0).astype(BF16)
                qaug[p, blocks(i, i + 1), :] = jnp.where(is_data, q_ref[blocks(i, i + 1), :], pen)

        for i in range(nb):
            sc = lax.dot_general(qaug[p, blocks(i, nb), :], kaug[p, blocks(i, i + 1), :],
                                 _CONTRACT_LAST, preferred_element_type=F32)
            for j in range(i, nb):
                s = sc[blocks(j - i, j - i + 1), :]
                if j - i < 2:
                    s = s + bias_ref[p, j - i]
                r0 = tile_row[i] + (j - i) * L
                sbuf[p, r0:r0 + L, :] = s

        head_out = []
        for j in range(nb):
            tiles = [tile_row[i] + (j - i) * L for i in range(j + 1)]
            m_run = None
            for r0 in tiles:
                s = sbuf[p, r0:r0 + L, :]
                t = jnp.maximum(s[:, :half], s[:, half:])
                m_run = t if m_run is None else jnp.maximum(m_run, t)
            m = jnp.max(m_run, axis=-1, keepdims=True)
            l_run = None
            acc = None
            for i, r0 in enumerate(tiles):
                e = jnp.exp(sbuf[p, r0:r0 + L, :] - m)
                t = e[:, :half] + e[:, half:]
                l_run = t if l_run is None else l_run + t
                pv = jnp.dot(e.astype(BF16), v_ref[blocks(i, i + 1), :], preferred_element_type=F32)
                acc = pv if acc is None else acc + pv
            head_out.append(acc / jnp.sum(l_run, axis=-1, keepdims=True))
        outs.append(head_out)

    for j in range(nb):
        o_ref[blocks(j, j + 1), :] = jnp.where(head0, outs[0][j], outs[1][j]).astype(BF16)


def _moba(mq, mk, mv, kmrep, bias, onehot, rankmat):
    B, S, _ = mq.shape
    nb = S // MOBA_BLOCK
    slab = pl.BlockSpec((None, S, LANES), lambda b, j: (b, 0, j))
    return pl.pallas_call(
        _moba_kernel,
        grid=(B, N_SLABS),
        in_specs=[slab, slab, slab,
                  pl.BlockSpec((None, HEADS_PER_SLAB, LANES, 2 * LANES), lambda b, j: (b, j, 0, 0)),
                  pl.BlockSpec((HEADS_PER_SLAB, 2, MOBA_BLOCK, MOBA_BLOCK), lambda b, j: (j, 0, 0, 0)),
                  _const_spec((HEADS_PER_SLAB, S, LANES)),
                  _const_spec((HEADS_PER_SLAB, LANES, LANES))],
        out_specs=slab,
        out_shape=jax.ShapeDtypeStruct((B, S, MOBA_WIDTH), BF16),
        scratch_shapes=[pltpu.VMEM((HEADS_PER_SLAB, S, LANES), BF16),
                        pltpu.VMEM((HEADS_PER_SLAB, S, LANES), BF16),
                        pltpu.VMEM((HEADS_PER_SLAB, nb * (nb + 1) // 2 * MOBA_BLOCK, MOBA_BLOCK), F32)],
        compiler_params=pltpu.CompilerParams(
            dimension_semantics=("arbitrary", "arbitrary"), vmem_limit_bytes=VMEM_LIMIT),
        name="moba",
    )(mq, mk, mv, kmrep, bias, onehot, rankmat)


def _mix_ffn_kernel(yr_ref, ym_ref, yc_ref, x_ref, wo_ref, g1_ref, b1_ref,
                    wu_ref, cw_ref, wd_ref, g2_ref, b2_ref, o_ref,
                    tail, x1_buf, x1b_buf):
    n_sub = x_ref.shape[0] // FFN_SUB_ROWS
    sub_rows = [slice(r * FFN_SUB_ROWS, (r + 1) * FFN_SUB_ROWS) for r in range(n_sub)]

    @pl.when(pl.program_id(1) == 0)
    def _():
        tail[...] = jnp.zeros((SUBLANES, D_FF), F32)

    for rs in sub_rows:
        cat = jnp.concatenate([yr_ref[rs, :], ym_ref[rs, :], yc_ref[rs, :]], axis=1)
        mix = jnp.dot(cat, wo_ref[...], preferred_element_type=F32)
        x1 = _layer_norm(DN_ALPHA * x_ref[rs, :] + mix, g1_ref[...], b1_ref[...])
        x1_buf[rs, :] = x1
        x1b_buf[rs, :] = x1.astype(BF16)

    subs = [(c0, min(c0 + FF_SUB, D_FF)) for c0 in range(0, D_FF, FF_SUB)]
    for rs in sub_rows:
        def up(c0, c1, rs=rs):
            x1b = x1b_buf[rs, :]
            return (jnp.dot(x1b, wu_ref[:, c0:c1], preferred_element_type=F32),
                    jnp.dot(x1b, wu_ref[:, D_FF + c0:D_FF + c1], preferred_element_type=F32))

        ahead = up(*subs[0])
        hs = []
        for n, (c0, c1) in enumerate(subs):
            a, gate = ahead
            if n + 1 < len(subs):
                ahead = up(*subs[n + 1])
            a = _causal_conv3(tail, slice(c0, c1), a, cw_ref, FFN_SUB_ROWS)
            h = 0.5 * a * (1.0 + lax.erf(a * math.sqrt(0.5))) * gate
            hs.append(h.astype(BF16))
        f = jnp.dot(jnp.concatenate(hs, axis=1), wd_ref[...], preferred_element_type=F32)
        o_ref[rs, :] = _layer_norm(DN_ALPHA * x1_buf[rs, :] + f, g2_ref[...], b2_ref[...])


def _mix_ffn(y_ret, y_moba, y_conv, x, w_out, g1, b1, w_up, ffn_cw, w_down, g2, b2):
    B, S, _ = x.shape
    row = lambda w: pl.BlockSpec((None, FFN_ROW_TILE, w), lambda b, t: (b, t, 0))
    return pl.pallas_call(
        _mix_ffn_kernel,
        grid=(B, S // FFN_ROW_TILE),
        in_specs=[
            row(RET_WIDTH), row(MOBA_WIDTH), row(CONV_WIDTH), row(D_MODEL),
            _const_spec((D_MODEL, D_MODEL)), _const_spec((1, D_MODEL)), _const_spec((1, D_MODEL)),
            _const_spec((D_MODEL, 2 * D_FF)), _const_spec((CONV_K, D_FF)),
            _const_spec((D_FF, D_MODEL)), _const_spec((1, D_MODEL)), _const_spec((1, D_MODEL)),
        ],
        out_specs=row(D_MODEL),
        out_shape=jax.ShapeDtypeStruct((B, S, D_MODEL), F32),
        scratch_shapes=[
            pltpu.VMEM((SUBLANES, D_FF), F32),
            pltpu.VMEM((FFN_ROW_TILE, D_MODEL), F32),
            pltpu.VMEM((FFN_ROW_TILE, D_MODEL), BF16),
        ],
        compiler_params=pltpu.CompilerParams(
            dimension_semantics=("arbitrary", "arbitrary"), vmem_limit_bytes=VMEM_LIMIT),
        name="mix_ffn",
    )(y_ret, y_moba, y_conv, x, w_out, g1, b1, w_up, ffn_cw, w_down, g2, b2)


def _t5_bucket(dist):
    n = jnp.maximum(dist, 0)
    max_exact = REL_BUCKETS // 2
    nf = jnp.maximum(n, 1).astype(F32)
    large = max_exact + (jnp.log(nf / max_exact) / math.log(REL_MAX_DIST / max_exact)
                         * (REL_BUCKETS - max_exact)).astype(jnp.int32)
    large = jnp.minimum(large, REL_BUCKETS - 1)
    return jnp.where(n < max_exact, n, large)


def _rotary_tables(S):
    inv = ROPE_BASE ** (-jnp.arange(0, HEAD_DIM, 2, dtype=F32) / HEAD_DIM)
    ang = jnp.arange(S, dtype=F32)[:, None] * inv[None, :]
    cos, sin = jnp.cos(ang), jnp.sin(ang)
    cos_t = jnp.tile(cos, (1, LANES // (HEAD_DIM // 2)))
    sin_t = jnp.tile(jnp.concatenate([-sin, sin], axis=-1), (1, HEADS_PER_SLAB))
    return cos_t, sin_t


def _retention_tables():
    C = RET_CHUNK
    log_g = jnp.log(1.0 - 2.0 ** (-5.0 - jnp.arange(RET_HEADS, dtype=F32)))
    idx = jnp.arange(C, dtype=F32)
    diff = idx[:, None] - idx[None, :]
    decay = jnp.where(diff >= 0, jnp.exp(jnp.maximum(diff, 0.0)[None] * log_g[:, None, None]), 0.0)
    xi = jnp.exp((idx + 1.0)[None, :] * log_g[:, None])
    zeta = jnp.exp((C - 1.0 - idx)[None, :] * log_g[:, None])
    g_chunk = jnp.exp(C * log_g)
    per_lane = lambda t: jnp.repeat(t.T, HEAD_DIM, axis=1)
    xi_t = per_lane(xi).reshape(C, N_SLABS, LANES).transpose(1, 0, 2)
    zeta_t = jnp.tile(per_lane(zeta), (ROW_TILE // C, 1))
    gc_t = jnp.broadcast_to(
        jnp.repeat(g_chunk, HEAD_DIM).reshape(N_SLABS, 1, LANES), (N_SLABS, LANES, LANES))
    return decay, xi_t, zeta_t, gc_t


def _bias_tables(rel_bias):
    L = MOBA_BLOCK
    rel_t = rel_bias.T.astype(F32)
    pos = jnp.arange(L, dtype=jnp.int32)
    d_own = pos[:, None] - pos[None, :]

    def lookup(dist):
        bucket = _t5_bucket(dist)
        out = jnp.zeros((MOBA_HEADS,) + dist.shape, F32)
        for b in range(REL_BUCKETS):
            out = jnp.where(bucket[None] == b, rel_t[:, b].reshape(-1, 1, 1), out)
        return out

    far = lookup(jnp.full((1, 1), 2 * L, jnp.int32))
    own = jnp.where(d_own[None] >= 0, lookup(d_own) - far, -jnp.inf)
    prev = lookup(d_own + L) - far
    return jnp.stack([own, prev], axis=1)


def _moba_tables(S):
    nb = S // MOBA_BLOCK
    assert nb <= MAX_BLOCKS
    onehot = np.zeros((HEADS_PER_SLAB, S, LANES), np.float32)
    rankmat = np.zeros((HEADS_PER_SLAB, LANES, LANES), np.float32)
    for p in range(HEADS_PER_SLAB):
        for r in range(S):
            onehot[p, r, _aug_lane(p, r // MOBA_BLOCK)] = 1.0
        for m in range(MAX_BLOCKS):
            for n in range(MAX_BLOCKS):
                rankmat[p, m * MAX_BLOCKS + n, _aug_lane(p, n)] = 1.0
    return jnp.asarray(onehot, BF16), jnp.asarray(rankmat, BF16)


def _gate_operand(km):
    B, nb, _ = km.shape
    kmt = km.reshape(B, nb, MOBA_HEADS, HEAD_DIM).transpose(0, 2, 3, 1)
    kmt = jnp.pad(kmt, ((0, 0), (0, 0), (0, 0), (0, MAX_BLOCKS - nb)))
    by_m = jnp.repeat(kmt, MAX_BLOCKS, axis=-1)
    by_n = jnp.tile(kmt, (1, 1, 1, MAX_BLOCKS))
    padc = lambda t: jnp.pad(t, ((0, 0), (0, 0), (0, 0), (0, LANES - t.shape[-1])))
    data = jnp.concatenate([padc(by_m), padc(by_n)], axis=-1)
    zeros = jnp.zeros_like(data)
    first = jnp.concatenate([data, zeros], axis=2)
    second = jnp.concatenate([zeros, data], axis=2)
    parity = (jnp.arange(MOBA_HEADS) % HEADS_PER_SLAB).reshape(1, -1, 1, 1)
    return jnp.where(parity == 0, first, second).astype(BF16)


def kernel(x, w_in, conv_w, w_out, ln1_g, ln1_b, w_up, ffn_conv_w, w_down, ln2_g, ln2_b, rel_bias):
    B, S, _ = x.shape
    cos_t, sin_t = _rotary_tables(S)
    decay, xi_t, zeta_t, gc_t = _retention_tables()
    bias = _bias_tables(rel_bias)
    onehot, rankmat = _moba_tables(S)
    for l in range(DEPTH):
        (qr, kr, kz, vr, gr, mq, mk, mv, km, yc) = _inproj(
            x, w_in[l].astype(BF16), cos_t, sin_t, zeta_t, conv_w[l])
        y_ret = _retention(qr, kr, kz, vr, gr, decay, xi_t, gc_t)
        kmrep = _gate_operand(km.reshape(B, S // MOBA_BLOCK, MOBA_WIDTH))
        y_moba = _moba(mq, mk, mv, kmrep, bias, onehot, rankmat)
        x = _mix_ffn(y_ret, y_moba, yc, x, w_out[l].astype(BF16),
                     ln1_g[l][None], ln1_b[l][None], w_up[l].astype(BF16), ffn_conv_w[l],
                     w_down[l].astype(BF16), ln2_g[l][None], ln2_b[l][None])
    return x
```

```python
import math

import numpy as np
import jax
import jax.numpy as jnp
from jax import lax
from jax.experimental import pallas as pl
from jax.experimental.pallas import tpu as pltpu

F32 = jnp.float32
BF16 = jnp.bfloat16

D_MODEL = 1024
HEAD_DIM = 64
RET_HEADS = 6
MOBA_HEADS = 6
RET_WIDTH = RET_HEADS * HEAD_DIM
MOBA_WIDTH = MOBA_HEADS * HEAD_DIM
CONV_WIDTH = 256
IN_COLS = 4 * RET_WIDTH + 3 * MOBA_WIDTH + 3 * CONV_WIDTH
CONV_K = 3
MOBA_BLOCK = 256
MOBA_TOPK = 3
REL_BUCKETS = 32
REL_MAX_DIST = 128
ROPE_BASE = 10000.0
D_FF = 2816
DEPTH = 2
DN_ALPHA = (2.0 * DEPTH) ** 0.25
LN_EPS = 1e-5
QK_SCALE = HEAD_DIM ** -0.5

LANES = 128
SUBLANES = 8
HEADS_PER_SLAB = LANES // HEAD_DIM
N_SLABS = RET_WIDTH // LANES
ROW_TILE = 1024
SUB_ROWS = 512
RET_CHUNK = 256
FF_SUB = 256
VMEM_LIMIT = 60 * 1024 * 1024
MASK_PENALTY = -1e30
MAX_BLOCKS = 8

_O = [0]
for _w in (RET_WIDTH,) * 4 + (MOBA_WIDTH,) * 3 + (CONV_WIDTH,) * 3:
    _O.append(_O[-1] + _w)
O_RQ, O_RK, O_RV, O_RG, O_MQ, O_MK, O_MV, O_CB, O_CC, O_CH, _ = _O

_CONTRACT_LAST = (((1,), (1,)), ((), ()))
_CONTRACT_FIRST = (((0,), (0,)), ((), ()))


def _const_spec(shape):
    zeros = (0,) * len(shape)
    return pl.BlockSpec(shape, lambda *_: zeros, pipeline_mode=pl.Buffered(1))


def _layer_spec(layer, shape):
    index = (layer,) + (0,) * len(shape)
    return pl.BlockSpec((None,) + tuple(shape), lambda *_: index, pipeline_mode=pl.Buffered(1))


def _layer_norm(v, g, b):
    mu = jnp.mean(v, axis=-1, keepdims=True)
    d = v - mu
    var = jnp.mean(d * d, axis=-1, keepdims=True)
    return d * lax.rsqrt(var + LN_EPS) * g + b


def _causal_conv3(tail, cols, cur, w_ref, rows):
    prev = tail[:, cols]
    sub = lax.broadcasted_iota(jnp.int32, prev.shape, 0)

    def shifted(k):
        body = pltpu.roll(cur, k, 0)
        head = jnp.where(sub < k, pltpu.roll(prev, k, 0), body[0:SUBLANES])
        return jnp.concatenate([head, body[SUBLANES:]], axis=0)

    y = shifted(2) * w_ref[0:1, cols]
    y = y + shifted(1) * w_ref[1:2, cols]
    y = y + cur * w_ref[2:3, cols]
    tail[:, cols] = cur[rows - SUBLANES:rows]
    return y


def _first_head_lanes(rows):
    return lax.broadcasted_iota(jnp.int32, (rows, LANES), 1) < HEAD_DIM


def _inproj_kernel(x_ref, w_ref, cos_ref, sin_ref, zeta_ref, cw_ref,
                   qr_o, kr_o, kz_o, vr_o, gr_o, mq_o, mk_o, mv_o, km_o, yc_o, pbuf):
    rows = SUB_ROWS
    lane = lax.broadcasted_iota(jnp.int32, (rows, LANES), 1)
    first_half = (lane & (HEAD_DIM // 2)) == 0

    @pl.when(pl.program_id(1) == 0)
    def _():
        pbuf[...] = jnp.zeros((SUBLANES, CONV_WIDTH), F32)

    for r in range(x_ref.shape[0] // rows):
        rs = slice(r * rows, (r + 1) * rows)
        xb = x_ref[rs, :].astype(BF16)

        def proj(c0, width, xb=xb):
            return jnp.dot(xb, w_ref[:, c0:c0 + width], preferred_element_type=F32)

        cos = cos_ref[rs, :]
        sin = sin_ref[rs, :]

        def rotary_slab(u, cos=cos, sin=sin):
            partner = jnp.where(first_half,
                                pltpu.roll(u, LANES - HEAD_DIM // 2, 1),
                                pltpu.roll(u, HEAD_DIM // 2, 1))
            return u * cos + partner * sin

        qk = proj(O_RQ, 2 * RET_WIDTH)
        for s in range(N_SLABS):
            cs = slice(s * LANES, (s + 1) * LANES)
            q = rotary_slab(qk[:, cs])
            qr_o[rs, cs] = q.astype(BF16)
            k = rotary_slab(qk[:, RET_WIDTH + s * LANES:RET_WIDTH + (s + 1) * LANES]) * QK_SCALE
            kr_o[rs, cs] = k.astype(BF16)
            kz_o[rs, cs] = (k * zeta_ref[rs, cs]).astype(BF16)

        vg = proj(O_RV, 2 * RET_WIDTH)
        vr_o[rs, :] = vg[:, :RET_WIDTH].astype(BF16)
        rg = vg[:, RET_WIDTH:]
        gr_o[rs, :] = rg * jax.nn.sigmoid(rg)
        mqk = proj(O_MQ, 2 * MOBA_WIDTH)
        mq_o[rs, :] = (mqk[:, :MOBA_WIDTH] * QK_SCALE).astype(BF16)
        mk = mqk[:, MOBA_WIDTH:]
        mk_o[rs, :] = mk.astype(BF16)
        nblk = rows // MOBA_BLOCK
        km_o[r * nblk:(r + 1) * nblk, :] = jnp.mean(
            mk.reshape(nblk, MOBA_BLOCK, MOBA_WIDTH), axis=1)
        rest = proj(O_MV, MOBA_WIDTH + 3 * CONV_WIDTH)
        mv_o[rs, :] = rest[:, :MOBA_WIDTH].astype(BF16)

        cb = rest[:, MOBA_WIDTH:MOBA_WIDTH + CONV_WIDTH]
        p = (rest[:, MOBA_WIDTH + CONV_WIDTH:MOBA_WIDTH + 2 * CONV_WIDTH]
             * rest[:, MOBA_WIDTH + 2 * CONV_WIDTH:])
        y = _causal_conv3(pbuf, slice(0, CONV_WIDTH), p, cw_ref, rows)
        yc_o[rs, :] = (cb * y).astype(BF16)


def _inproj(layer, x, w_in, cos, sin, zeta, conv_w):
    B, S, _ = x.shape
    nt = S // ROW_TILE
    nblk = ROW_TILE // MOBA_BLOCK
    row = lambda w: pl.BlockSpec((None, ROW_TILE, w), lambda b, t: (b, t, 0))
    sds = jax.ShapeDtypeStruct
    return pl.pallas_call(
        _inproj_kernel,
        grid=(B, nt),
        in_specs=[
            row(D_MODEL),
            _layer_spec(layer, (D_MODEL, IN_COLS)),
            pl.BlockSpec((ROW_TILE, LANES), lambda b, t: (t, 0)),
            pl.BlockSpec((ROW_TILE, LANES), lambda b, t: (t, 0)),
            _const_spec((ROW_TILE, RET_WIDTH)),
            _layer_spec(layer, (CONV_K, CONV_WIDTH)),
        ],
        out_specs=[
            row(RET_WIDTH), row(RET_WIDTH), row(RET_WIDTH), row(RET_WIDTH), row(RET_WIDTH),
            row(MOBA_WIDTH), row(MOBA_WIDTH), row(MOBA_WIDTH),
            pl.BlockSpec((None, None, nblk, MOBA_WIDTH), lambda b, t: (b, t, 0, 0)),
            row(CONV_WIDTH),
        ],
        out_shape=[
            sds((B, S, RET_WIDTH), BF16), sds((B, S, RET_WIDTH), BF16),
            sds((B, S, RET_WIDTH), BF16), sds((B, S, RET_WIDTH), BF16),
            sds((B, S, RET_WIDTH), F32),
            sds((B, S, MOBA_WIDTH), BF16), sds((B, S, MOBA_WIDTH), BF16),
            sds((B, S, MOBA_WIDTH), BF16),
            sds((B, nt, nblk, MOBA_WIDTH), F32),
            sds((B, S, CONV_WIDTH), BF16),
        ],
        scratch_shapes=[pltpu.VMEM((SUBLANES, CONV_WIDTH), F32)],
        compiler_params=pltpu.CompilerParams(
            dimension_semantics=("arbitrary", "arbitrary"), vmem_limit_bytes=VMEM_LIMIT),
        name="inproj",
    )(x, w_in, cos, sin, zeta, conv_w)


def _retention_kernel(q_ref, k_ref, kz_ref, v_ref, g_ref, dec_ref, xi_ref, gc_ref, o_ref):
    C = RET_CHUNK
    n_chunks = q_ref.shape[0] // C
    head0 = _first_head_lanes(C)
    r = lax.broadcasted_iota(jnp.int32, (LANES, LANES), 0) < HEAD_DIM
    c = lax.broadcasted_iota(jnp.int32, (LANES, LANES), 1) < HEAD_DIM
    same_head = r == c
    inv_dim = 1.0 / HEAD_DIM

    def head_mean(t):
        s0 = jnp.sum(jnp.where(head0, t, 0.0), axis=-1, keepdims=True)
        s1 = jnp.sum(jnp.where(head0, 0.0, t), axis=-1, keepdims=True)
        return jnp.where(head0, s0, s1) * inv_dim

    state = jnp.zeros((LANES, LANES), F32)
    for ci in range(n_chunks):
        rs = slice(ci * C, (ci + 1) * C)
        q = q_ref[rs, :]
        k = k_ref[rs, :]
        v = v_ref[rs, :]
        inner = []
        for hh in range(HEADS_PER_SLAB):
            kh = jnp.where(head0 if hh == 0 else jnp.logical_not(head0), k, jnp.zeros_like(k))
            scores = lax.dot_general(q, kh, _CONTRACT_LAST, preferred_element_type=F32) * dec_ref[hh]
            inner.append(jnp.dot(scores.astype(BF16), v, preferred_element_type=F32))
        inner = jnp.where(head0, inner[0], inner[1])
        cross = jnp.dot(q, state.astype(BF16), preferred_element_type=F32) * xi_ref[...]
        o = inner + cross
        kv = lax.dot_general(kz_ref[rs, :], v, _CONTRACT_FIRST, preferred_element_type=F32)
        state = state * gc_ref[...] + jnp.where(same_head, kv, 0.0)
        d = o - head_mean(o)
        var = head_mean(d * d)
        y = d * lax.rsqrt(var + LN_EPS) * g_ref[rs, :]
        o_ref[rs, :] = y.astype(BF16)


def _retention(qr, kr, kz, vr, gr, decay, xi, gc):
    B, S, _ = qr.shape
    slab = pl.BlockSpec((None, S, LANES), lambda b, j: (b, 0, j))
    return pl.pallas_call(
        _retention_kernel,
        grid=(B, N_SLABS),
        in_specs=[slab, slab, slab, slab, slab,
                  pl.BlockSpec((HEADS_PER_SLAB, RET_CHUNK, RET_CHUNK), lambda b, j: (j, 0, 0)),
                  pl.BlockSpec((None, RET_CHUNK, LANES), lambda b, j: (j, 0, 0)),
                  pl.BlockSpec((None, LANES, LANES), lambda b, j: (j, 0, 0))],
        out_specs=slab,
        out_shape=jax.ShapeDtypeStruct((B, S, RET_WIDTH), BF16),
        compiler_params=pltpu.CompilerParams(
            dimension_semantics=("arbitrary", "arbitrary"), vmem_limit_bytes=VMEM_LIMIT),
        name="retention",
    )(qr, kr, kz, vr, gr, decay, xi, gc)


def _aug_lane(p, n):
    return (HEAD_DIM if p == 0 else 0) + n


def _moba_kernel(q_ref, k_ref, v_ref, km_ref, bias_ref, oh_ref, rk_ref, o_ref, qaug, kaug, sbuf):
    L = MOBA_BLOCK
    S = q_ref.shape[0]
    nb = S // L
    half = L // 2
    first_ranked = min(MOBA_TOPK + 1, nb)
    head0 = _first_head_lanes(L)
    lane = lax.broadcasted_iota(jnp.int32, (L, LANES), 1)
    pair_m = lane >> 3
    pair_n = lane & 7
    m_before_n = pair_m < pair_n
    blocks = lambda i0, i1: slice(i0 * L, i1 * L)
    tile_row = [0]
    for i in range(nb):
        tile_row.append(tile_row[-1] + (nb - i) * L)

    def augment_operands(p):
        is_data = head0 if p == 0 else jnp.logical_not(head0)
        for i in range(nb):
            kaug[p, blocks(i, i + 1), :] = jnp.where(is_data, k_ref[blocks(i, i + 1), :],
                                                     oh_ref[p, blocks(i, i + 1), :])
        for i in range(first_ranked):
            qaug[p, blocks(i, i + 1), :] = jnp.where(is_data, q_ref[blocks(i, i + 1), :],
                                                     jnp.zeros((L, LANES), BF16))
        if first_ranked < nb:
            g = jnp.dot(q_ref[blocks(first_ranked, nb), :], km_ref[p], preferred_element_type=F32)
            beats = []
            for i in range(first_ranked, nb):
                gi = g[blocks(i - first_ranked, i - first_ranked + 1), :]
                gm = gi[:, :LANES]
                gn = gi[:, LANES:]
                valid = (lane < MAX_BLOCKS * MAX_BLOCKS) & (pair_m < i) & (pair_n < i)
                b = ((gm > gn) | ((gm == gn) & m_before_n)) & valid
                beats.append(jnp.where(b, 1.0, 0.0).astype(BF16))
            rank = jnp.dot(jnp.concatenate(beats, axis=0), rk_ref[p], preferred_element_type=F32)
            for i in range(first_ranked, nb):
                r = rank[blocks(i - first_ranked, i - first_ranked + 1), :]
                pen = jnp.where(r >= MOBA_TOPK, MASK_PENALTY, 0.0).astype(BF16)
                qaug[p, blocks(i, i + 1), :] = jnp.where(is_data, q_ref[blocks(i, i + 1), :], pen)

    def logits_of_key_block(p, i):
        sc = lax.dot_general(qaug[p, blocks(i, nb), :], kaug[p, blocks(i, i + 1), :],
                             _CONTRACT_LAST, preferred_element_type=F32)
        for j in range(i, nb):
            s = sc[blocks(j - i, j - i + 1), :]
            if j - i < 2:
                s = s + bias_ref[p, j - i]
            r0 = tile_row[i] + (j - i) * L
            sbuf[p, r0:r0 + L, :] = s

    def attend_query_block(p, j):
        tiles = [tile_row[i] + (j - i) * L for i in range(j + 1)]
        m_run = None
        for r0 in tiles:
            s = sbuf[p, r0:r0 + L, :]
            t = jnp.maximum(s[:, :half], s[:, half:])
            m_run = t if m_run is None else jnp.maximum(m_run, t)
        m = jnp.max(m_run, axis=-1, keepdims=True)
        l_run = None
        acc = None
        for i, r0 in enumerate(tiles):
            e = jnp.exp(sbuf[p, r0:r0 + L, :] - m)
            t = e[:, :half] + e[:, half:]
            l_run = t if l_run is None else l_run + t
            pv = jnp.dot(e.astype(BF16), v_ref[blocks(i, i + 1), :], preferred_element_type=F32)
            acc = pv if acc is None else acc + pv
        return acc / jnp.sum(l_run, axis=-1, keepdims=True)

    outs = []
    for p in range(HEADS_PER_SLAB):
        augment_operands(p)
        for i in range(nb):
            logits_of_key_block(p, i)
        outs.append([attend_query_block(p, j) for j in range(nb)])
    for j in range(nb):
        o_ref[blocks(j, j + 1), :] = jnp.where(head0, outs[0][j], outs[1][j]).astype(BF16)


def _moba(mq, mk, mv, kmrep, bias, onehot, rankmat):
    B, S, _ = mq.shape
    nb = S // MOBA_BLOCK
    slab = pl.BlockSpec((None, S, LANES), lambda b, j: (b, 0, j))
    return pl.pallas_call(
        _moba_kernel,
        grid=(B, N_SLABS),
        in_specs=[slab, slab, slab,
                  pl.BlockSpec((None, HEADS_PER_SLAB, LANES, 2 * LANES), lambda b, j: (b, j, 0, 0)),
                  pl.BlockSpec((HEADS_PER_SLAB, 2, MOBA_BLOCK, MOBA_BLOCK), lambda b, j: (j, 0, 0, 0)),
                  _const_spec((HEADS_PER_SLAB, S, LANES)),
                  _const_spec((HEADS_PER_SLAB, LANES, LANES))],
        out_specs=slab,
        out_shape=jax.ShapeDtypeStruct((B, S, MOBA_WIDTH), BF16),
        scratch_shapes=[pltpu.VMEM((HEADS_PER_SLAB, S, LANES), BF16),
                        pltpu.VMEM((HEADS_PER_SLAB, S, LANES), BF16),
                        pltpu.VMEM((HEADS_PER_SLAB, nb * (nb + 1) // 2 * MOBA_BLOCK, MOBA_BLOCK), F32)],
        compiler_params=pltpu.CompilerParams(
            dimension_semantics=("arbitrary", "arbitrary"), vmem_limit_bytes=VMEM_LIMIT),
        name="moba",
    )(mq, mk, mv, kmrep, bias, onehot, rankmat)


def _mix_ffn_kernel(yr_ref, ym_ref, yc_ref, x_ref, wo_ref, g1_ref, b1_ref,
                    wu_ref, cw_ref, wd_ref, g2_ref, b2_ref, o_ref,
                    tail, x1_buf, x1b_buf):
    n_sub = x_ref.shape[0] // SUB_ROWS
    sub_rows = [slice(r * SUB_ROWS, (r + 1) * SUB_ROWS) for r in range(n_sub)]

    @pl.when(pl.program_id(1) == 0)
    def _():
        tail[...] = jnp.zeros((SUBLANES, D_FF), F32)

    for rs in sub_rows:
        cat = jnp.concatenate([yr_ref[rs, :], ym_ref[rs, :], yc_ref[rs, :]], axis=1)
        mix = jnp.dot(cat, wo_ref[...], preferred_element_type=F32)
        x1 = _layer_norm(DN_ALPHA * x_ref[rs, :] + mix, g1_ref[...], b1_ref[...])
        x1_buf[rs, :] = x1
        x1b_buf[rs, :] = x1.astype(BF16)

    subs = [(c0, min(c0 + FF_SUB, D_FF)) for c0 in range(0, D_FF, FF_SUB)]
    for rs in sub_rows:
        def up(c0, c1, rs=rs):
            x1b = x1b_buf[rs, :]
            return (jnp.dot(x1b, wu_ref[:, c0:c1], preferred_element_type=F32),
                    jnp.dot(x1b, wu_ref[:, D_FF + c0:D_FF + c1], preferred_element_type=F32))

        ahead = up(*subs[0])
        hs = []
        for n, (c0, c1) in enumerate(subs):
            a, gate = ahead
            if n + 1 < len(subs):
                ahead = up(*subs[n + 1])
            a = _causal_conv3(tail, slice(c0, c1), a, cw_ref, SUB_ROWS)
            h = 0.5 * a * (1.0 + lax.erf(a * math.sqrt(0.5))) * gate
            hs.append(h.astype(BF16))
        f = jnp.dot(jnp.concatenate(hs, axis=1), wd_ref[...], preferred_element_type=F32)
        o_ref[rs, :] = _layer_norm(DN_ALPHA * x1_buf[rs, :] + f, g2_ref[...], b2_ref[...])


def _mix_ffn(layer, y_ret, y_moba, y_conv, x, w_out, g1, b1, w_up, ffn_cw, w_down, g2, b2):
    B, S, _ = x.shape
    row = lambda w: pl.BlockSpec((None, ROW_TILE, w), lambda b, t: (b, t, 0))
    return pl.pallas_call(
        _mix_ffn_kernel,
        grid=(B, S // ROW_TILE),
        in_specs=[
            row(RET_WIDTH), row(MOBA_WIDTH), row(CONV_WIDTH), row(D_MODEL),
            _layer_spec(layer, (D_MODEL, D_MODEL)),
            _layer_spec(layer, (1, D_MODEL)), _layer_spec(layer, (1, D_MODEL)),
            _layer_spec(layer, (D_MODEL, 2 * D_FF)), _layer_spec(layer, (CONV_K, D_FF)),
            _layer_spec(layer, (D_FF, D_MODEL)),
            _layer_spec(layer, (1, D_MODEL)), _layer_spec(layer, (1, D_MODEL)),
        ],
        out_specs=row(D_MODEL),
        out_shape=jax.ShapeDtypeStruct((B, S, D_MODEL), F32),
        scratch_shapes=[
            pltpu.VMEM((SUBLANES, D_FF), F32),
            pltpu.VMEM((ROW_TILE, D_MODEL), F32),
            pltpu.VMEM((ROW_TILE, D_MODEL), BF16),
        ],
        compiler_params=pltpu.CompilerParams(
            dimension_semantics=("arbitrary", "arbitrary"), vmem_limit_bytes=VMEM_LIMIT),
        name="mix_ffn",
    )(y_ret, y_moba, y_conv, x, w_out, g1, b1, w_up, ffn_cw, w_down, g2, b2)


def _t5_bucket(dist):
    n = jnp.maximum(dist, 0)
    max_exact = REL_BUCKETS // 2
    nf = jnp.maximum(n, 1).astype(F32)
    large = max_exact + (jnp.log(nf / max_exact) / math.log(REL_MAX_DIST / max_exact)
                         * (REL_BUCKETS - max_exact)).astype(jnp.int32)
    large = jnp.minimum(large, REL_BUCKETS - 1)
    return jnp.where(n < max_exact, n, large)


def _rotary_tables(S):
    inv = ROPE_BASE ** (-jnp.arange(0, HEAD_DIM, 2, dtype=F32) / HEAD_DIM)
    ang = jnp.arange(S, dtype=F32)[:, None] * inv[None, :]
    cos, sin = jnp.cos(ang), jnp.sin(ang)
    cos_t = jnp.tile(cos, (1, LANES // (HEAD_DIM // 2)))
    sin_t = jnp.tile(jnp.concatenate([-sin, sin], axis=-1), (1, HEADS_PER_SLAB))
    return cos_t, sin_t


def _retention_tables():
    C = RET_CHUNK
    log_g = jnp.log(1.0 - 2.0 ** (-5.0 - jnp.arange(RET_HEADS, dtype=F32)))
    idx = jnp.arange(C, dtype=F32)
    diff = idx[:, None] - idx[None, :]
    decay = jnp.where(diff >= 0, jnp.exp(jnp.maximum(diff, 0.0)[None] * log_g[:, None, None]), 0.0)
    xi = jnp.exp((idx + 1.0)[None, :] * log_g[:, None])
    zeta = jnp.exp((C - 1.0 - idx)[None, :] * log_g[:, None])
    g_chunk = jnp.exp(C * log_g)
    per_lane = lambda t: jnp.repeat(t.T, HEAD_DIM, axis=1)
    xi_t = per_lane(xi).reshape(C, N_SLABS, LANES).transpose(1, 0, 2)
    zeta_t = jnp.tile(per_lane(zeta), (ROW_TILE // C, 1))
    gc_t = jnp.broadcast_to(
        jnp.repeat(g_chunk, HEAD_DIM).reshape(N_SLABS, 1, LANES), (N_SLABS, LANES, LANES))
    return decay, xi_t, zeta_t, gc_t


def _bias_tables(rel_bias):
    L = MOBA_BLOCK
    rel_t = rel_bias.T.astype(F32)
    pos = jnp.arange(L, dtype=jnp.int32)
    d_own = pos[:, None] - pos[None, :]

    def lookup(dist):
        bucket = _t5_bucket(dist)
        out = jnp.zeros((MOBA_HEADS,) + dist.shape, F32)
        for b in range(REL_BUCKETS):
            out = jnp.where(bucket[None] == b, rel_t[:, b].reshape(-1, 1, 1), out)
        return out

    far = lookup(jnp.full((1, 1), 2 * L, jnp.int32))
    own = jnp.where(d_own[None] >= 0, lookup(d_own) - far, -jnp.inf)
    prev = lookup(d_own + L) - far
    return jnp.stack([own, prev], axis=1)


def _moba_tables(S):
    nb = S // MOBA_BLOCK
    assert nb <= MAX_BLOCKS
    onehot = np.zeros((HEADS_PER_SLAB, S, LANES), np.float32)
    rankmat = np.zeros((HEADS_PER_SLAB, LANES, LANES), np.float32)
    for p in range(HEADS_PER_SLAB):
        for r in range(S):
            onehot[p, r, _aug_lane(p, r // MOBA_BLOCK)] = 1.0
        for m in range(MAX_BLOCKS):
            for n in range(MAX_BLOCKS):
                rankmat[p, m * MAX_BLOCKS + n, _aug_lane(p, n)] = 1.0
    return jnp.asarray(onehot, BF16), jnp.asarray(rankmat, BF16)


def _gate_operand(km):
    B, nb, _ = km.shape
    kmt = km.reshape(B, nb, MOBA_HEADS, HEAD_DIM).transpose(0, 2, 3, 1)
    kmt = jnp.pad(kmt, ((0, 0), (0, 0), (0, 0), (0, MAX_BLOCKS - nb)))
    by_m = jnp.repeat(kmt, MAX_BLOCKS, axis=-1)
    by_n = jnp.tile(kmt, (1, 1, 1, MAX_BLOCKS))
    padc = lambda t: jnp.pad(t, ((0, 0), (0, 0), (0, 0), (0, LANES - t.shape[-1])))
    data = jnp.concatenate([padc(by_m), padc(by_n)], axis=-1)
    zeros = jnp.zeros_like(data)
    first = jnp.concatenate([data, zeros], axis=2)
    second = jnp.concatenate([zeros, data], axis=2)
    parity = (jnp.arange(MOBA_HEADS) % HEADS_PER_SLAB).reshape(1, -1, 1, 1)
    return jnp.where(parity == 0, first, second).astype(BF16)


def kernel(x, w_in, conv_w, w_out, ln1_g, ln1_b, w_up, ffn_conv_w, w_down, ln2_g, ln2_b, rel_bias):
    B, S, _ = x.shape
    cos_t, sin_t = _rotary_tables(S)
    decay, xi_t, zeta_t, gc_t = _retention_tables()
    bias = _bias_tables(rel_bias)
    onehot, rankmat = _moba_tables(S)
    w_in, w_out, w_up, w_down = (w.astype(BF16) for w in (w_in, w_out, w_up, w_down))
    ln1_g, ln1_b, ln2_g, ln2_b = (t[:, None, :] for t in (ln1_g, ln1_b, ln2_g, ln2_b))
    for l in range(DEPTH):
        (qr, kr, kz, vr, gr, mq, mk, mv, km, yc) = _inproj(
            l, x, w_in, cos_t, sin_t, zeta_t, conv_w)
        y_ret = _retention(qr, kr, kz, vr, gr, decay, xi_t, gc_t)
        kmrep = _gate_operand(km.reshape(B, S // MOBA_BLOCK, MOBA_WIDTH))
        y_moba = _moba(mq, mk, mv, kmrep, bias, onehot, rankmat)
        x = _mix_ffn(l, y_ret, y_moba, yc, x, w_out, ln1_g, ln1_b, w_up, ffn_conv_w,
                     w_down, ln2_g, ln2_b)
    return x
```

```python
import math

import numpy as np
import jax
import jax.numpy as jnp
from jax import lax
from jax.experimental import pallas as pl
from jax.experimental.pallas import tpu as pltpu

F32 = jnp.float32
BF16 = jnp.bfloat16

D_MODEL = 1024
HEAD_DIM = 64
RET_HEADS = 6
MOBA_HEADS = 6
RET_WIDTH = RET_HEADS * HEAD_DIM
MOBA_WIDTH = MOBA_HEADS * HEAD_DIM
CONV_WIDTH = 256
IN_COLS = 4 * RET_WIDTH + 3 * MOBA_WIDTH + 3 * CONV_WIDTH
CONV_K = 3
MOBA_BLOCK = 256
MOBA_TOPK = 3
REL_BUCKETS = 32
REL_MAX_DIST = 128
ROPE_BASE = 10000.0
D_FF = 2816
DEPTH = 2
DN_ALPHA = (2.0 * DEPTH) ** 0.25
LN_EPS = 1e-5
QK_SCALE = HEAD_DIM ** -0.5

LANES = 128
SUBLANES = 8
HEADS_PER_SLAB = LANES // HEAD_DIM
N_SLABS = RET_WIDTH // LANES
ROW_TILE = 1024
SUB_ROWS = 512
RET_CHUNK = 256
FF_SUB = 256
VMEM_LIMIT = 60 * 1024 * 1024
MASK_PENALTY = -1e30
MAX_BLOCKS = 8

_O = [0]
for _w in (RET_WIDTH,) * 4 + (MOBA_WIDTH,) * 3 + (CONV_WIDTH,) * 3:
    _O.append(_O[-1] + _w)
O_RQ, O_RK, O_RV, O_RG, O_MQ, O_MK, O_MV, O_CB, O_CC, O_CH, _ = _O

_CONTRACT_LAST = (((1,), (1,)), ((), ()))
_CONTRACT_FIRST = (((0,), (0,)), ((), ()))


def _const_spec(shape):
    zeros = (0,) * len(shape)
    return pl.BlockSpec(shape, lambda *_: zeros, pipeline_mode=pl.Buffered(1))


def _layer_spec(layer, shape):
    index = (layer,) + (0,) * len(shape)
    return pl.BlockSpec((None,) + tuple(shape), lambda *_: index, pipeline_mode=pl.Buffered(1))


def _layer_norm(v, g, b):
    mu = jnp.mean(v, axis=-1, keepdims=True)
    d = v - mu
    var = jnp.mean(d * d, axis=-1, keepdims=True)
    return d * lax.rsqrt(var + LN_EPS) * g + b


def _causal_conv3(tail, cols, cur, w_ref, rows):
    prev = tail[:, cols]
    sub = lax.broadcasted_iota(jnp.int32, prev.shape, 0)

    def shifted(k):
        body = pltpu.roll(cur, k, 0)
        head = jnp.where(sub < k, pltpu.roll(prev, k, 0), body[0:SUBLANES])
        return jnp.concatenate([head, body[SUBLANES:]], axis=0)

    y = shifted(2) * w_ref[0:1, cols]
    y = y + shifted(1) * w_ref[1:2, cols]
    y = y + cur * w_ref[2:3, cols]
    tail[:, cols] = cur[rows - SUBLANES:rows]
    return y


def _first_head_lanes(rows):
    return lax.broadcasted_iota(jnp.int32, (rows, LANES), 1) < HEAD_DIM


def _inproj_kernel(x_ref, w_ref, cos_ref, sin_ref, zeta_ref, cw_ref,
                   qr_o, kr_o, kz_o, vr_o, gr_o, mq_o, mk_o, mv_o, km_o, yc_o, pbuf):
    rows = SUB_ROWS
    lane = lax.broadcasted_iota(jnp.int32, (rows, LANES), 1)
    first_half = (lane & (HEAD_DIM // 2)) == 0

    @pl.when(pl.program_id(1) == 0)
    def _():
        pbuf[...] = jnp.zeros((SUBLANES, CONV_WIDTH), F32)

    for r in range(x_ref.shape[0] // rows):
        rs = slice(r * rows, (r + 1) * rows)
        xb = x_ref[rs, :].astype(BF16)

        def proj(c0, width, xb=xb):
            return jnp.dot(xb, w_ref[:, c0:c0 + width], preferred_element_type=F32)

        cos = cos_ref[rs, :]
        sin = sin_ref[rs, :]

        def rotary_slab(u, cos=cos, sin=sin):
            partner = jnp.where(first_half,
                                pltpu.roll(u, LANES - HEAD_DIM // 2, 1),
                                pltpu.roll(u, HEAD_DIM // 2, 1))
            return u * cos + partner * sin

        qk = proj(O_RQ, 2 * RET_WIDTH)
        for s in range(N_SLABS):
            cs = slice(s * LANES, (s + 1) * LANES)
            q = rotary_slab(qk[:, cs])
            qr_o[rs, cs] = q.astype(BF16)
            k = rotary_slab(qk[:, RET_WIDTH + s * LANES:RET_WIDTH + (s + 1) * LANES]) * QK_SCALE
            kr_o[rs, cs] = k.astype(BF16)
            kz_o[rs, cs] = (k * zeta_ref[rs, cs]).astype(BF16)

        vg = proj(O_RV, 2 * RET_WIDTH)
        vr_o[rs, :] = vg[:, :RET_WIDTH].astype(BF16)
        rg = vg[:, RET_WIDTH:]
        gr_o[rs, :] = rg * jax.nn.sigmoid(rg)
        mqk = proj(O_MQ, 2 * MOBA_WIDTH)
        mq_o[rs, :] = (mqk[:, :MOBA_WIDTH] * QK_SCALE).astype(BF16)
        mk = mqk[:, MOBA_WIDTH:]
        mk_o[rs, :] = mk.astype(BF16)
        nblk = rows // MOBA_BLOCK
        km_o[r * nblk:(r + 1) * nblk, :] = jnp.mean(
            mk.reshape(nblk, MOBA_BLOCK, MOBA_WIDTH), axis=1)
        rest = proj(O_MV, MOBA_WIDTH + 3 * CONV_WIDTH)
        mv_o[rs, :] = rest[:, :MOBA_WIDTH].astype(BF16)

        cb = rest[:, MOBA_WIDTH:MOBA_WIDTH + CONV_WIDTH]
        p = (rest[:, MOBA_WIDTH + CONV_WIDTH:MOBA_WIDTH + 2 * CONV_WIDTH]
             * rest[:, MOBA_WIDTH + 2 * CONV_WIDTH:])
        y = _causal_conv3(pbuf, slice(0, CONV_WIDTH), p, cw_ref, rows)
        yc_o[rs, :] = (cb * y).astype(BF16)


def _inproj(layer, x, w_in, cos, sin, zeta, conv_w):
    B, S, _ = x.shape
    nt = S // ROW_TILE
    nblk = ROW_TILE // MOBA_BLOCK
    row = lambda w: pl.BlockSpec((None, ROW_TILE, w), lambda b, t: (b, t, 0))
    sds = jax.ShapeDtypeStruct
    return pl.pallas_call(
        _inproj_kernel,
        grid=(B, nt),
        in_specs=[
            row(D_MODEL),
            _layer_spec(layer, (D_MODEL, IN_COLS)),
            pl.BlockSpec((ROW_TILE, LANES), lambda b, t: (t, 0)),
            pl.BlockSpec((ROW_TILE, LANES), lambda b, t: (t, 0)),
            _const_spec((ROW_TILE, RET_WIDTH)),
            _layer_spec(layer, (CONV_K, CONV_WIDTH)),
        ],
        out_specs=[
            row(RET_WIDTH), row(RET_WIDTH), row(RET_WIDTH), row(RET_WIDTH), row(RET_WIDTH),
            row(MOBA_WIDTH), row(MOBA_WIDTH), row(MOBA_WIDTH),
            pl.BlockSpec((None, None, nblk, MOBA_WIDTH), lambda b, t: (b, t, 0, 0)),
            row(CONV_WIDTH),
        ],
        out_shape=[
            sds((B, S, RET_WIDTH), BF16), sds((B, S, RET_WIDTH), BF16),
            sds((B, S, RET_WIDTH), BF16), sds((B, S, RET_WIDTH), BF16),
            sds((B, S, RET_WIDTH), F32),
            sds((B, S, MOBA_WIDTH), BF16), sds((B, S, MOBA_WIDTH), BF16),
            sds((B, S, MOBA_WIDTH), BF16),
            sds((B, nt, nblk, MOBA_WIDTH), F32),
            sds((B, S, CONV_WIDTH), BF16),
        ],
        scratch_shapes=[pltpu.VMEM((SUBLANES, CONV_WIDTH), F32)],
        compiler_params=pltpu.CompilerParams(
            dimension_semantics=("arbitrary", "arbitrary"), vmem_limit_bytes=VMEM_LIMIT),
        name="inproj",
    )(x, w_in, cos, sin, zeta, conv_w)


def _retention_kernel(q_ref, k_ref, kz_ref, v_ref, g_ref, dec_ref, xi_ref, gc_ref, o_ref):
    C = RET_CHUNK
    n_chunks = q_ref.shape[0] // C
    head0 = _first_head_lanes(C)
    r = lax.broadcasted_iota(jnp.int32, (LANES, LANES), 0) < HEAD_DIM
    c = lax.broadcasted_iota(jnp.int32, (LANES, LANES), 1) < HEAD_DIM
    same_head = r == c
    inv_dim = 1.0 / HEAD_DIM

    def head_mean(t):
        s0 = jnp.sum(jnp.where(head0, t, 0.0), axis=-1, keepdims=True)
        s1 = jnp.sum(jnp.where(head0, 0.0, t), axis=-1, keepdims=True)
        return jnp.where(head0, s0, s1) * inv_dim

    states = [jnp.zeros((LANES, LANES), F32) for _ in range(N_SLABS)]
    for ci in range(n_chunks):
        rs = slice(ci * C, (ci + 1) * C)
        for s in range(N_SLABS):
            cs = slice(s * LANES, (s + 1) * LANES)
            q = q_ref[rs, cs]
            k = k_ref[rs, cs]
            v = v_ref[rs, cs]
            inner = []
            for hh in range(HEADS_PER_SLAB):
                kh = jnp.where(head0 if hh == 0 else jnp.logical_not(head0), k, jnp.zeros_like(k))
                scores = lax.dot_general(q, kh, _CONTRACT_LAST, preferred_element_type=F32)
                scores = scores * dec_ref[s * HEADS_PER_SLAB + hh]
                inner.append(jnp.dot(scores.astype(BF16), v, preferred_element_type=F32))
            inner = jnp.where(head0, inner[0], inner[1])
            cross = jnp.dot(q, states[s].astype(BF16), preferred_element_type=F32) * xi_ref[s]
            o = inner + cross
            kv = lax.dot_general(kz_ref[rs, cs], v, _CONTRACT_FIRST, preferred_element_type=F32)
            states[s] = states[s] * gc_ref[s] + jnp.where(same_head, kv, 0.0)
            d = o - head_mean(o)
            var = head_mean(d * d)
            y = d * lax.rsqrt(var + LN_EPS) * g_ref[rs, cs]
            o_ref[rs, cs] = y.astype(BF16)


def _retention(qr, kr, kz, vr, gr, decay, xi, gc):
    B, S, _ = qr.shape
    seq = pl.BlockSpec((None, S, RET_WIDTH), lambda b: (b, 0, 0))
    return pl.pallas_call(
        _retention_kernel,
        grid=(B,),
        in_specs=[seq, seq, seq, seq, seq,
                  _const_spec((RET_HEADS, RET_CHUNK, RET_CHUNK)),
                  _const_spec((N_SLABS, RET_CHUNK, LANES)),
                  _const_spec((N_SLABS, LANES, LANES))],
        out_specs=seq,
        out_shape=jax.ShapeDtypeStruct((B, S, RET_WIDTH), BF16),
        compiler_params=pltpu.CompilerParams(
            dimension_semantics=("arbitrary",), vmem_limit_bytes=VMEM_LIMIT),
        name="retention",
    )(qr, kr, kz, vr, gr, decay, xi, gc)


def _aug_lane(p, n):
    return (HEAD_DIM if p == 0 else 0) + n


def _moba_kernel(q_ref, k_ref, v_ref, km_ref, bias_ref, oh_ref, rk_ref, o_ref, qaug, kaug, sbuf):
    L = MOBA_BLOCK
    S = q_ref.shape[0]
    nb = S // L
    half = L // 2
    first_ranked = min(MOBA_TOPK + 1, nb)
    head0 = _first_head_lanes(L)
    lane = lax.broadcasted_iota(jnp.int32, (L, LANES), 1)
    pair_m = lane >> 3
    pair_n = lane & 7
    m_before_n = pair_m < pair_n
    blocks = lambda i0, i1: slice(i0 * L, i1 * L)
    tile_row = [0]
    for i in range(nb):
        tile_row.append(tile_row[-1] + (nb - i) * L)

    def augment_operands(p):
        is_data = head0 if p == 0 else jnp.logical_not(head0)
        for i in range(nb):
            kaug[p, blocks(i, i + 1), :] = jnp.where(is_data, k_ref[blocks(i, i + 1), :],
                                                     oh_ref[p, blocks(i, i + 1), :])
        for i in range(first_ranked):
            qaug[p, blocks(i, i + 1), :] = jnp.where(is_data, q_ref[blocks(i, i + 1), :],
                                                     jnp.zeros((L, LANES), BF16))
        if first_ranked < nb:
            g = jnp.dot(q_ref[blocks(first_ranked, nb), :], km_ref[p], preferred_element_type=F32)
            beats = []
            for i in range(first_ranked, nb):
                gi = g[blocks(i - first_ranked, i - first_ranked + 1), :]
                gm = gi[:, :LANES]
                gn = gi[:, LANES:]
                valid = (lane < MAX_BLOCKS * MAX_BLOCKS) & (pair_m < i) & (pair_n < i)
                b = ((gm > gn) | ((gm == gn) & m_before_n)) & valid
                beats.append(jnp.where(b, 1.0, 0.0).astype(BF16))
            rank = jnp.dot(jnp.concatenate(beats, axis=0), rk_ref[p], preferred_element_type=F32)
            for i in range(first_ranked, nb):
                r = rank[blocks(i - first_ranked, i - first_ranked + 1), :]
                pen = jnp.where(r >= MOBA_TOPK, MASK_PENALTY, 0.0).astype(BF16)
                qaug[p, blocks(i, i + 1), :] = jnp.where(is_data, q_ref[blocks(i, i + 1), :], pen)

    def logits_of_key_block(p, i):
        sc = lax.dot_general(qaug[p, blocks(i, nb), :], kaug[p, blocks(i, i + 1), :],
                             _CONTRACT_LAST, preferred_element_type=F32)
        for j in range(i, nb):
            s = sc[blocks(j - i, j - i + 1), :]
            if j - i < 2:
                s = s + bias_ref[p, j - i]
            r0 = tile_row[i] + (j - i) * L
            sbuf[p, r0:r0 + L, :] = s

    def attend_query_block(p, j):
        tiles = [tile_row[i] + (j - i) * L for i in range(j + 1)]
        m_run = None
        for r0 in tiles:
            s = sbuf[p, r0:r0 + L, :]
            t = jnp.maximum(s[:, :half], s[:, half:])
            m_run = t if m_run is None else jnp.maximum(m_run, t)
        m = jnp.max(m_run, axis=-1, keepdims=True)
        l_run = None
        acc = None
        for i, r0 in enumerate(tiles):
            e = jnp.exp(sbuf[p, r0:r0 + L, :] - m)
            t = e[:, :half] + e[:, half:]
            l_run = t if l_run is None else l_run + t
            pv = jnp.dot(e.astype(BF16), v_ref[blocks(i, i + 1), :], preferred_element_type=F32)
            acc = pv if acc is None else acc + pv
        return acc / jnp.sum(l_run, axis=-1, keepdims=True)

    outs = []
    for p in range(HEADS_PER_SLAB):
        augment_operands(p)
        for i in range(nb):
            logits_of_key_block(p, i)
        outs.append([attend_query_block(p, j) for j in range(nb)])
    for j in range(nb):
        o_ref[blocks(j, j + 1), :] = jnp.where(head0, outs[0][j], outs[1][j]).astype(BF16)


def _moba(mq, mk, mv, kmrep, bias, onehot, rankmat):
    B, S, _ = mq.shape
    nb = S // MOBA_BLOCK
    slab = pl.BlockSpec((None, S, LANES), lambda b, j: (b, 0, j))
    return pl.pallas_call(
        _moba_kernel,
        grid=(B, N_SLABS),
        in_specs=[slab, slab, slab,
                  pl.BlockSpec((None, HEADS_PER_SLAB, LANES, 2 * LANES), lambda b, j: (b, j, 0, 0)),
                  pl.BlockSpec((HEADS_PER_SLAB, 2, MOBA_BLOCK, MOBA_BLOCK), lambda b, j: (j, 0, 0, 0)),
                  _const_spec((HEADS_PER_SLAB, S, LANES)),
                  _const_spec((HEADS_PER_SLAB, LANES, LANES))],
        out_specs=slab,
        out_shape=jax.ShapeDtypeStruct((B, S, MOBA_WIDTH), BF16),
        scratch_shapes=[pltpu.VMEM((HEADS_PER_SLAB, S, LANES), BF16),
                        pltpu.VMEM((HEADS_PER_SLAB, S, LANES), BF16),
                        pltpu.VMEM((HEADS_PER_SLAB, nb * (nb + 1) // 2 * MOBA_BLOCK, MOBA_BLOCK), F32)],
        compiler_params=pltpu.CompilerParams(
            dimension_semantics=("arbitrary", "arbitrary"), vmem_limit_bytes=VMEM_LIMIT),
        name="moba",
    )(mq, mk, mv, kmrep, bias, onehot, rankmat)


def _mix_ffn_kernel(yr_ref, ym_ref, yc_ref, x_ref, wo_ref, g1_ref, b1_ref,
                    wu_ref, cw_ref, wd_ref, g2_ref, b2_ref, o_ref,
                    tail, x1_buf, x1b_buf):
    n_sub = x_ref.shape[0] // SUB_ROWS
    sub_rows = [slice(r * SUB_ROWS, (r + 1) * SUB_ROWS) for r in range(n_sub)]

    @pl.when(pl.program_id(1) == 0)
    def _():
        tail[...] = jnp.zeros((SUBLANES, D_FF), F32)

    for rs in sub_rows:
        cat = jnp.concatenate([yr_ref[rs, :], ym_ref[rs, :], yc_ref[rs, :]], axis=1)
        mix = jnp.dot(cat, wo_ref[...], preferred_element_type=F32)
        x1 = _layer_norm(DN_ALPHA * x_ref[rs, :] + mix, g1_ref[...], b1_ref[...])
        x1_buf[rs, :] = x1
        x1b_buf[rs, :] = x1.astype(BF16)

    subs = [(c0, min(c0 + FF_SUB, D_FF)) for c0 in range(0, D_FF, FF_SUB)]
    for rs in sub_rows:
        def up(c0, c1, rs=rs):
            x1b = x1b_buf[rs, :]
            return (jnp.dot(x1b, wu_ref[:, c0:c1], preferred_element_type=F32),
                    jnp.dot(x1b, wu_ref[:, D_FF + c0:D_FF + c1], preferred_element_type=F32))

        ahead = up(*subs[0])
        hs = []
        for n, (c0, c1) in enumerate(subs):
            a, gate = ahead
            if n + 1 < len(subs):
                ahead = up(*subs[n + 1])
            a = _causal_conv3(tail, slice(c0, c1), a, cw_ref, SUB_ROWS)
            h = 0.5 * a * (1.0 + lax.erf(a * math.sqrt(0.5))) * gate
            hs.append(h.astype(BF16))
        f = jnp.dot(jnp.concatenate(hs, axis=1), wd_ref[...], preferred_element_type=F32)
        o_ref[rs, :] = _layer_norm(DN_ALPHA * x1_buf[rs, :] + f, g2_ref[...], b2_ref[...])


def _mix_ffn(layer, y_ret, y_moba, y_conv, x, w_out, g1, b1, w_up, ffn_cw, w_down, g2, b2):
    B, S, _ = x.shape
    row = lambda w: pl.BlockSpec((None, ROW_TILE, w), lambda b, t: (b, t, 0))
    return pl.pallas_call(
        _mix_ffn_kernel,
        grid=(B, S // ROW_TILE),
        in_specs=[
            row(RET_WIDTH), row(MOBA_WIDTH), row(CONV_WIDTH), row(D_MODEL),
            _layer_spec(layer, (D_MODEL, D_MODEL)),
            _layer_spec(layer, (1, D_MODEL)), _layer_spec(layer, (1, D_MODEL)),
            _layer_spec(layer, (D_MODEL, 2 * D_FF)), _layer_spec(layer, (CONV_K, D_FF)),
            _layer_spec(layer, (D_FF, D_MODEL)),
            _layer_spec(layer, (1, D_MODEL)), _layer_spec(layer, (1, D_MODEL)),
        ],
        out_specs=row(D_MODEL),
        out_shape=jax.ShapeDtypeStruct((B, S, D_MODEL), F32),
        scratch_shapes=[
            pltpu.VMEM((SUBLANES, D_FF), F32),
            pltpu.VMEM((ROW_TILE, D_MODEL), F32),
            pltpu.VMEM((ROW_TILE, D_MODEL), BF16),
        ],
        compiler_params=pltpu.CompilerParams(
            dimension_semantics=("arbitrary", "arbitrary"), vmem_limit_bytes=VMEM_LIMIT),
        name="mix_ffn",
    )(y_ret, y_moba, y_conv, x, w_out, g1, b1, w_up, ffn_cw, w_down, g2, b2)


def _t5_bucket(dist):
    n = jnp.maximum(dist, 0)
    max_exact = REL_BUCKETS // 2
    nf = jnp.maximum(n, 1).astype(F32)
    large = max_exact + (jnp.log(nf / max_exact) / math.log(REL_MAX_DIST / max_exact)
                         * (REL_BUCKETS - max_exact)).astype(jnp.int32)
    large = jnp.minimum(large, REL_BUCKETS - 1)
    return jnp.where(n < max_exact, n, large)


def _rotary_tables(S):
    inv = ROPE_BASE ** (-jnp.arange(0, HEAD_DIM, 2, dtype=F32) / HEAD_DIM)
    ang = jnp.arange(S, dtype=F32)[:, None] * inv[None, :]
    cos, sin = jnp.cos(ang), jnp.sin(ang)
    cos_t = jnp.tile(cos, (1, LANES // (HEAD_DIM // 2)))
    sin_t = jnp.tile(jnp.concatenate([-sin, sin], axis=-1), (1, HEADS_PER_SLAB))
    return cos_t, sin_t


def _retention_tables():
    C = RET_CHUNK
    log_g = jnp.log(1.0 - 2.0 ** (-5.0 - jnp.arange(RET_HEADS, dtype=F32)))
    idx = jnp.arange(C, dtype=F32)
    diff = idx[:, None] - idx[None, :]
    decay = jnp.where(diff >= 0, jnp.exp(jnp.maximum(diff, 0.0)[None] * log_g[:, None, None]), 0.0)
    xi = jnp.exp((idx + 1.0)[None, :] * log_g[:, None])
    zeta = jnp.exp((C - 1.0 - idx)[None, :] * log_g[:, None])
    g_chunk = jnp.exp(C * log_g)
    per_lane = lambda t: jnp.repeat(t.T, HEAD_DIM, axis=1)
    xi_t = per_lane(xi).reshape(C, N_SLABS, LANES).transpose(1, 0, 2)
    zeta_t = jnp.tile(per_lane(zeta), (ROW_TILE // C, 1))
    gc_t = jnp.broadcast_to(
        jnp.repeat(g_chunk, HEAD_DIM).reshape(N_SLABS, 1, LANES), (N_SLABS, LANES, LANES))
    return decay, xi_t, zeta_t, gc_t


def _bias_tables(rel_bias):
    L = MOBA_BLOCK
    rel_t = rel_bias.T.astype(F32)
    pos = jnp.arange(L, dtype=jnp.int32)
    d_own = pos[:, None] - pos[None, :]

    def lookup(dist):
        bucket = _t5_bucket(dist)
        out = jnp.zeros((MOBA_HEADS,) + dist.shape, F32)
        for b in range(REL_BUCKETS):
            out = jnp.where(bucket[None] == b, rel_t[:, b].reshape(-1, 1, 1), out)
        return out

    far = lookup(jnp.full((1, 1), 2 * L, jnp.int32))
    own = jnp.where(d_own[None] >= 0, lookup(d_own) - far, -jnp.inf)
    prev = lookup(d_own + L) - far
    return jnp.stack([own, prev], axis=1)


def _moba_tables(S):
    nb = S // MOBA_BLOCK
    assert nb <= MAX_BLOCKS
    onehot = np.zeros((HEADS_PER_SLAB, S, LANES), np.float32)
    rankmat = np.zeros((HEADS_PER_SLAB, LANES, LANES), np.float32)
    for p in range(HEADS_PER_SLAB):
        for r in range(S):
            onehot[p, r, _aug_lane(p, r // MOBA_BLOCK)] = 1.0
        for m in range(MAX_BLOCKS):
            for n in range(MAX_BLOCKS):
                rankmat[p, m * MAX_BLOCKS + n, _aug_lane(p, n)] = 1.0
    return jnp.asarray(onehot, BF16), jnp.asarray(rankmat, BF16)


def _gate_operand(km):
    B, nb, _ = km.shape
    kmt = km.reshape(B, nb, MOBA_HEADS, HEAD_DIM).transpose(0, 2, 3, 1)
    kmt = jnp.pad(kmt, ((0, 0), (0, 0), (0, 0), (0, MAX_BLOCKS - nb)))
    by_m = jnp.repeat(kmt, MAX_BLOCKS, axis=-1)
    by_n = jnp.tile(kmt, (1, 1, 1, MAX_BLOCKS))
    padc = lambda t: jnp.pad(t, ((0, 0), (0, 0), (0, 0), (0, LANES - t.shape[-1])))
    data = jnp.concatenate([padc(by_m), padc(by_n)], axis=-1)
    zeros = jnp.zeros_like(data)
    first = jnp.concatenate([data, zeros], axis=2)
    second = jnp.concatenate([zeros, data], axis=2)
    parity = (jnp.arange(MOBA_HEADS) % HEADS_PER_SLAB).reshape(1, -1, 1, 1)
    return jnp.where(parity == 0, first, second).astype(BF16)


def kernel(x, w_in, conv_w, w_out, ln1_g, ln1_b, w_up, ffn_conv_w, w_down, ln2_g, ln2_b, rel_bias):
    B, S, _ = x.shape
    cos_t, sin_t = _rotary_tables(S)
    decay, xi_t, zeta_t, gc_t = _retention_tables()
    bias = _bias_tables(rel_bias)
    onehot, rankmat = _moba_tables(S)
    w_in, w_out, w_up, w_down = (w.astype(BF16) for w in (w_in, w_out, w_up, w_down))
    ln1_g, ln1_b, ln2_g, ln2_b = (t[:, None, :] for t in (ln1_g, ln1_b, ln2_g, ln2_b))
    for l in range(DEPTH):
        (qr, kr, kz, vr, gr, mq, mk, mv, km, yc) = _inproj(
            l, x, w_in, cos_t, sin_t, zeta_t, conv_w)
        y_ret = _retention(qr, kr, kz, vr, gr, decay, xi_t, gc_t)
        kmrep = _gate_operand(km.reshape(B, S // MOBA_BLOCK, MOBA_WIDTH))
        y_moba = _moba(mq, mk, mv, kmrep, bias, onehot, rankmat)
        x = _mix_ffn(l, y_ret, y_moba, yc, x, w_out, ln1_g, ln1_b, w_up, ffn_conv_w,
                     w_down, ln2_g, ln2_b)
    return x
```
